```python
import math
import jax
import jax.numpy as jnp
from jax import lax
import numpy as np

D_MODEL = 1024
BATCH = 8
SEQ = 2048
DEPTH = 4
DEC_BATCH = 32
DEC_SEQ = 8
PAST_LEN = 8192
PAGE_SIZE = 128

N_MIXERS = 2
N_A_LAYERS = (DEPTH + 1) // 2
N_B_LAYERS = DEPTH // 2
HEAD_DIM = 64
A_HEADS = D_MODEL // HEAD_DIM
B_HEADS = D_MODEL // (2 * HEAD_DIM)
ROT_DIM = HEAD_DIM // 4
ROPE_THETA = 500000.0
MOBA_BLOCK = 256
MOBA_TOPK = 3
D_FF = 4 * D_MODEL
EPS = 1e-6
Q_CHUNK_A = 16
Q_CHUNK_B = 128

kernel_name = 'moba_diffattn_hybrid_step'


def rms_norm(x, g):
    xf = x.astype(jnp.float32)
    y = xf * lax.rsqrt(jnp.mean(xf * xf, axis=-1, keepdims=True) + EPS)
    return (y * g.astype(jnp.float32)).astype(x.dtype)


def rotary(x, pos):
    half = ROT_DIM // 2
    inv_freq = ROPE_THETA ** (-jnp.arange(half, dtype=jnp.float32) * (2.0 / ROT_DIM))
    ang = pos.astype(jnp.float32)[:, None] * inv_freq[None, :]
    cos = jnp.cos(ang)[:, None, :]
    sin = jnp.sin(ang)[:, None, :]
    xr = x[..., :ROT_DIM].astype(jnp.float32)
    x1, x2 = xr[..., :half], xr[..., half:]
    rot = jnp.concatenate([x1 * cos - x2 * sin, x2 * cos + x1 * sin], axis=-1)
    return jnp.concatenate([rot.astype(x.dtype), x[..., ROT_DIM:]], axis=-1)


def gather_pages(cache, layer, page_table):
    g = cache[layer, page_table]
    return g.reshape(page_table.shape[0], -1, cache.shape[3], cache.shape[4])


def sqrelu_mlp(h, w_up, w_down):
    u = jax.nn.relu(h @ w_up)
    return (u * u) @ w_down


def moba_attention(q, k, v, q_pos):
    b, sq, h, dh = q.shape
    sk = k.shape[1]
    n_blk = -(-sk // MOBA_BLOCK)
    pad = n_blk * MOBA_BLOCK - sk
    kb = jnp.pad(k, ((0, 0), (0, pad), (0, 0), (0, 0))).reshape(b, n_blk, MOBA_BLOCK, h, dh).transpose(0, 3, 1, 2, 4)
    vb = jnp.pad(v, ((0, 0), (0, pad), (0, 0), (0, 0))).reshape(b, n_blk, MOBA_BLOCK, h, dh).transpose(0, 3, 1, 2, 4)
    k_mean = jnp.mean(kb.astype(jnp.float32), axis=3)
    n_sel = min(MOBA_TOPK, n_blk)
    qc = Q_CHUNK_A if sq % Q_CHUNK_A == 0 else sq
    n_chunks = sq // qc
    qh = q.reshape(b, n_chunks, qc, h, dh).transpose(1, 0, 3, 2, 4)
    pos_c = q_pos.reshape(n_chunks, qc)
    bi = jnp.arange(b)[:, None, None, None]
    hi = jnp.arange(h)[None, :, None, None]
    blk_ids = jnp.arange(n_blk)
    in_blk = jnp.arange(MOBA_BLOCK)
    scale = dh ** -0.5

    def one_chunk(args):
        qq, pp = args
        own = pp // MOBA_BLOCK
        gate = jnp.einsum('bhqd,bhnd->bhqn', qq.astype(jnp.float32), k_mean)
        gate = jnp.where(blk_ids[None, :] < own[:, None], gate, -jnp.inf)
        _, top_idx = lax.top_k(gate, n_sel)
        sel_ok = top_idx < own[None, None, :, None]
        idx = jnp.concatenate([top_idx, jnp.broadcast_to(own[None, None, :, None], (b, h, qc, 1))], axis=-1)
        valid_slot = jnp.concatenate([sel_ok, jnp.ones((b, h, qc, 1), dtype=bool)], axis=-1)
        key_pos = idx[..., None] * MOBA_BLOCK + in_blk
        ok = valid_slot[..., None] & (key_pos <= pp[None, None, :, None, None])
        k_g = kb[bi, hi, idx]
        v_g = vb[bi, hi, idx]
        logits = jnp.einsum('bhqd,bhqjkd->bhqjk', qq, k_g, preferred_element_type=jnp.float32) * scale
        logits = jnp.where(ok, logits, -jnp.inf)
        p = jax.nn.softmax(logits.reshape(b, h, qc, -1), axis=-1)
        o = jnp.einsum('bhqm,bhqmd->bhqd', p, v_g.reshape(b, h, qc, -1, dh))
        return o.astype(q.dtype)

    out = lax.map(one_chunk, (qh, pos_c))
    return out.transpose(1, 0, 3, 2, 4).reshape(b, sq, h, dh)


def diff_attention(q, k, v, q_pos, lam):
    b, sq, h, _, dh = q.shape
    sk = k.shape[1]
    qc = Q_CHUNK_B if sq % Q_CHUNK_B == 0 else sq
    n_chunks = sq // qc
    qh = q.reshape(b, n_chunks, qc, h, 2, dh).transpose(1, 0, 3, 4, 2, 5)
    pos_c = q_pos.reshape(n_chunks, qc)
    key_pos = jnp.arange(sk)
    scale = dh ** -0.5

    def one_chunk(args):
        qq, pp = args
        s = jnp.einsum('bhcqd,bkhcd->bhcqk', qq, k, preferred_element_type=jnp.float32) * scale
        s = jnp.where(key_pos[None, :] <= pp[:, None], s, -jnp.inf)
        p = jax.nn.softmax(s, axis=-1)
        w = p[:, :, 0] - lam * p[:, :, 1]
        return jnp.einsum('bhqk,bkhe->bqhe', w, v)

    out = lax.map(one_chunk, (qh, pos_c))
    return out.reshape(b, sq, h, 2 * dh)


def moba_mixer(h, pos, past_k, past_v, w_qkv, w_o, g_q, g_k):
    b, s, _ = h.shape
    q, k, v = jnp.split(h @ w_qkv, 3, axis=-1)
    q = rotary(rms_norm(q.reshape(b, s, A_HEADS, HEAD_DIM), g_q), pos)
    k = rotary(rms_norm(k.reshape(b, s, A_HEADS, HEAD_DIM), g_k), pos)
    v = v.reshape(b, s, A_HEADS, HEAD_DIM)
    if past_k is None:
        k_all, v_all = k, v
    else:
        k_all = jnp.concatenate([past_k.astype(k.dtype), k], axis=1)
        v_all = jnp.concatenate([past_v.astype(v.dtype), v], axis=1)
    o = moba_attention(q, k_all, v_all, pos)
    return o.reshape(b, s, D_MODEL) @ w_o, k, v


def diff_mixer(h, pos, past_k, past_v, w_qkv, w_o, g_q, g_k, lam_vec, g_sub, lam_init):
    b, s, _ = h.shape
    q, k, v = jnp.split(h @ w_qkv, 3, axis=-1)
    q = rotary(rms_norm(q.reshape(b, s, 2 * B_HEADS, HEAD_DIM), g_q), pos).reshape(b, s, B_HEADS, 2, HEAD_DIM)
    k = rotary(rms_norm(k.reshape(b, s, 2 * B_HEADS, HEAD_DIM), g_k), pos).reshape(b, s, B_HEADS, 2 * HEAD_DIM)
    v = v.reshape(b, s, B_HEADS, 2 * HEAD_DIM)
    if past_k is None:
        k_all, v_all = k, v
    else:
        k_all = jnp.concatenate([past_k.astype(k.dtype), k], axis=1)
        v_all = jnp.concatenate([past_v.astype(v.dtype), v], axis=1)
    lv = lam_vec.astype(jnp.float32)
    lam = jnp.exp(jnp.sum(lv[0] * lv[1])) - jnp.exp(jnp.sum(lv[2] * lv[3])) + lam_init
    o = diff_attention(q, k_all.reshape(b, -1, B_HEADS, 2, HEAD_DIM), v_all, pos, lam)
    o = (rms_norm(o, g_sub) * (1.0 - lam_init)).astype(h.dtype)
    return o.reshape(b, s, D_MODEL) @ w_o, k, v


def setup_inputs(seed: int = 0):
    key = jax.random.key(seed)
    ks = jax.random.split(key, 24)
    f32 = jnp.float32
    d = D_MODEL
    n_pages = PAST_LEN // PAGE_SIZE
    n_used = DEC_BATCH * n_pages
    n_phys = n_used + max(1, n_used // 4)

    def nrm(k, shape, scale):
        return jax.random.normal(k, shape, f32) * scale

    def gain(k, shape):
        return 1.0 + 0.02 * jax.random.normal(k, shape, f32)

    page_table = jax.random.permutation(ks[6], n_phys)[:n_used].reshape(DEC_BATCH, n_pages).astype(jnp.int32)
    return {
        'x_prompt': nrm(ks[0], (BATCH, SEQ, d), 1.0),
        'x_sample': nrm(ks[1], (DEC_BATCH, DEC_SEQ, d), 1.0),
        'cache_moba_k': nrm(ks[2], (N_A_LAYERS, n_phys, PAGE_SIZE, A_HEADS, HEAD_DIM), 1.0),
        'cache_moba_v': nrm(ks[3], (N_A_LAYERS, n_phys, PAGE_SIZE, A_HEADS, HEAD_DIM), 1.0),
        'cache_diff_k': nrm(ks[4], (N_B_LAYERS, n_phys, PAGE_SIZE, B_HEADS, 2 * HEAD_DIM), 1.0),
        'cache_diff_v': nrm(ks[5], (N_B_LAYERS, n_phys, PAGE_SIZE, B_HEADS, 2 * HEAD_DIM), 1.0),
        'page_table': page_table,
        'norm_mix': gain(ks[7], (DEPTH, d)),
        'norm_mlp': gain(ks[8], (DEPTH, d)),
        'moba_w_qkv': nrm(ks[9], (N_A_LAYERS, d, 3 * d), d ** -0.5),
        'moba_w_o': nrm(ks[10], (N_A_LAYERS, d, d), d ** -0.5),
        'moba_q_norm': gain(ks[11], (N_A_LAYERS, HEAD_DIM)),
        'moba_k_norm': gain(ks[12], (N_A_LAYERS, HEAD_DIM)),
        'diff_w_qkv': nrm(ks[13], (N_B_LAYERS, d, 3 * d), d ** -0.5),
        'diff_w_o': nrm(ks[14], (N_B_LAYERS, d, d), d ** -0.5),
        'diff_q_norm': gain(ks[15], (N_B_LAYERS, HEAD_DIM)),
        'diff_k_norm': gain(ks[16], (N_B_LAYERS, HEAD_DIM)),
        'diff_lambda': nrm(ks[17], (N_B_LAYERS, 4, HEAD_DIM), 0.1),
        'diff_subln': gain(ks[18], (N_B_LAYERS, 2 * HEAD_DIM)),
        'mlp_w_up': nrm(ks[19], (DEPTH, d, D_FF), d ** -0.5),
        'mlp_w_down': nrm(ks[20], (DEPTH, D_FF, d), D_FF ** -0.5),
    }


def reference(x_prompt, x_sample, cache_moba_k, cache_moba_v, cache_diff_k, cache_diff_v, page_table,
              norm_mix, norm_mlp, moba_w_qkv, moba_w_o, moba_q_norm, moba_k_norm,
              diff_w_qkv, diff_w_o, diff_q_norm, diff_k_norm, diff_lambda, diff_subln,
              mlp_w_up, mlp_w_down):
    past_len = page_table.shape[1] * cache_moba_k.shape[2]
    pos_p = jnp.arange(x_prompt.shape[1])
    pos_s = past_len + jnp.arange(x_sample.shape[1])
    xp, xs = x_prompt, x_sample
    mk_p, mv_p, mk_s, mv_s = [], [], [], []
    dk_p, dv_p, dk_s, dv_s = [], [], [], []
    for i in range(DEPTH):
        j = i // N_MIXERS
        hp = rms_norm(xp, norm_mix[i])
        hs = rms_norm(xs, norm_mix[i])
        if i % N_MIXERS == 0:
            w = (moba_w_qkv[j], moba_w_o[j], moba_q_norm[j], moba_k_norm[j])
            op, kp, vp = moba_mixer(hp, pos_p, None, None, *w)
            os_, ks_, vs_ = moba_mixer(hs, pos_s, gather_pages(cache_moba_k, j, page_table),
                                       gather_pages(cache_moba_v, j, page_table), *w)
            mk_p.append(kp); mv_p.append(vp); mk_s.append(ks_); mv_s.append(vs_)
        else:
            lam_init = 0.8 - 0.6 * math.exp(-0.3 * i)
            w = (diff_w_qkv[j], diff_w_o[j], diff_q_norm[j], diff_k_norm[j], diff_lambda[j], diff_subln[j], lam_init)
            op, kp, vp = diff_mixer(hp, pos_p, None, None, *w)
            os_, ks_, vs_ = diff_mixer(hs, pos_s, gather_pages(cache_diff_k, j, page_table),
                                       gather_pages(cache_diff_v, j, page_table), *w)
            dk_p.append(kp); dv_p.append(vp); dk_s.append(ks_); dv_s.append(vs_)
        xp = xp + op
        xs = xs + os_
        xp = xp + sqrelu_mlp(rms_norm(xp, norm_mlp[i]), mlp_w_up[i], mlp_w_down[i])
        xs = xs + sqrelu_mlp(rms_norm(xs, norm_mlp[i]), mlp_w_up[i], mlp_w_down[i])
    return (xp, xs, jnp.stack(mk_p), jnp.stack(mv_p), jnp.stack(dk_p), jnp.stack(dv_p),
            jnp.stack(mk_s), jnp.stack(mv_s), jnp.stack(dk_s), jnp.stack(dv_s))
```

```python
import functools
import math

import jax
import jax.numpy as jnp
from jax import lax
from jax.experimental import pallas as pl
from jax.experimental.pallas import tpu as pltpu

F32 = jnp.float32
BF16 = jnp.bfloat16

D_MODEL = 1024
HEAD_DIM = 64
ROT_DIM = HEAD_DIM // 4
ROPE_THETA = 500000.0
MOBA_BLOCK = 256
MOBA_TOPK = 3
Q_CHUNK_DIFF = 128
EPS = 1e-6
N_MIXERS = 2
LANES = 128
SCALE = HEAD_DIM ** -0.5
NEG = -1e30
VMEM_LIMIT = 56 * 1024 * 1024

_NT = (((1,), (1,)), ((), ()))


def _dot(a, b):
    return jnp.dot(a, b, preferred_element_type=F32)


def _dot_nt(a, b):
    return lax.dot_general(a, b, _NT, preferred_element_type=F32)


def _split_bf16(x):
    hi = x.astype(BF16)
    lo = (x - hi.astype(F32)).astype(BF16)
    return hi, lo


def _params(sem):
    return pltpu.CompilerParams(dimension_semantics=sem, vmem_limit_bytes=VMEM_LIMIT)


def _const_spec(shape):
    zeros = (0,) * len(shape)
    return pl.BlockSpec(shape, lambda *_: zeros, pipeline_mode=pl.Buffered(1))


def _lane_half(shape):
    return lax.broadcasted_iota(jnp.int32, shape, len(shape) - 1) // HEAD_DIM


def _qkv_kernel(x_ref, g_ref, w_ref, gq_ref, gk_ref, cos_ref, sa_ref, sb_ref,
                q_ref, k_ref, v_ref, *, transpose_kv):
    x = x_ref[...]
    ms = jnp.mean(x * x, axis=-1, keepdims=True)
    h = (x * lax.rsqrt(ms + EPS) * g_ref[...]).astype(BF16)
    y = _dot(h, w_ref[...])
    cos, sa, sb = cos_ref[...], sa_ref[...], sb_ref[...]
    first = _lane_half((1, LANES)) == 0

    def norm_rope(blk, gain):
        sq = blk * blk
        s0 = jnp.sum(jnp.where(first, sq, 0.0), axis=-1, keepdims=True)
        s1 = jnp.sum(jnp.where(first, 0.0, sq), axis=-1, keepdims=True)
        r = lax.rsqrt(jnp.where(first, s0, s1) * (1.0 / HEAD_DIM) + EPS)
        n = blk * r * gain
        return n * cos + pltpu.roll(n, LANES - ROT_DIM // 2, 1) * sa + pltpu.roll(n, ROT_DIM // 2, 1) * sb

    n_tiles = D_MODEL // LANES
    for c in range(n_tiles):
        sl = slice(c * LANES, (c + 1) * LANES)
        q_ref[:, sl] = norm_rope(y[:, sl], gq_ref[...])
        kb = norm_rope(y[:, D_MODEL + c * LANES:D_MODEL + (c + 1) * LANES], gk_ref[...])
        vb = y[:, 2 * D_MODEL + c * LANES:2 * D_MODEL + (c + 1) * LANES]
        if transpose_kv:
            k_ref[sl, :] = kb.T
            v_ref[sl, :] = vb.T
        else:
            k_ref[:, sl] = kb
            v_ref[:, sl] = vb


def _qkv_proj(x, g, w, gq, gk, rope, *, seq_tiles, transpose_kv, tm):
    t = x.shape[0]
    n_tiles = t // tm
    cos, sa, sb = rope
    row_spec = pl.BlockSpec((tm, D_MODEL), lambda i: (i, 0))
    rope_spec = pl.BlockSpec((tm, LANES), lambda i: (i % seq_tiles, 0))
    if transpose_kv:
        kv_shape = jax.ShapeDtypeStruct((n_tiles // seq_tiles, D_MODEL, seq_tiles * tm), F32)
        kv_spec = pl.BlockSpec((None, D_MODEL, tm), lambda i: (i // seq_tiles, 0, i % seq_tiles))
    else:
        kv_shape = jax.ShapeDtypeStruct((t, D_MODEL), F32)
        kv_spec = row_spec
    return pl.pallas_call(
        functools.partial(_qkv_kernel, transpose_kv=transpose_kv),
        grid=(n_tiles,),
        in_specs=[row_spec, _const_spec((1, D_MODEL)), _const_spec((D_MODEL, 3 * D_MODEL)),
                  _const_spec((1, LANES)), _const_spec((1, LANES)), rope_spec, rope_spec, rope_spec],
        out_specs=[row_spec, kv_spec, kv_spec],
        out_shape=[jax.ShapeDtypeStruct((t, D_MODEL), F32), kv_shape, kv_shape],
        compiler_params=_params(("parallel",)),
        name="qkv_proj_t" if transpose_kv else "qkv_proj",
    )(x, g, w, gq, gk, cos, sa, sb)


def _rope_tables(pos):
    half = ROT_DIM // 2
    inv_freq = ROPE_THETA ** (-jnp.arange(half, dtype=F32) * (2.0 / ROT_DIM))
    ang = pos.astype(F32)[:, None] * inv_freq[None, :]
    lane = jnp.arange(LANES)
    d = lane % HEAD_DIM
    cos_l = jnp.cos(ang)[:, lane % half]
    sin_l = jnp.sin(ang)[:, lane % half]
    cos = jnp.where(d < ROT_DIM, cos_l, 1.0)
    sa = jnp.where(d < half, -sin_l, 0.0)
    sb = jnp.where((d >= half) & (d < ROT_DIM), sin_l, 0.0)
    return cos.astype(F32), sa.astype(F32), sb.astype(F32)


def _mlp_kernel(x_ref, o_ref, wo_ref, g_ref, wup_ref, wdn_ref, out_ref, *, ff_chunk):
    x1 = x_ref[...] + _dot(o_ref[...].astype(BF16), wo_ref[...])
    ms = jnp.mean(x1 * x1, axis=-1, keepdims=True)
    h = (x1 * lax.rsqrt(ms + EPS) * g_ref[...]).astype(BF16)
    acc = x1
    for c in range(wup_ref.shape[1] // ff_chunk):
        u = jnp.maximum(_dot(h, wup_ref[:, c * ff_chunk:(c + 1) * ff_chunk]), 0.0)
        acc = acc + _dot((u * u).astype(BF16), wdn_ref[c * ff_chunk:(c + 1) * ff_chunk, :])
    out_ref[...] = acc


def _out_mlp(x, o, wo, g, wup, wdn, *, tm):
    t = x.shape[0]
    d_ff = wup.shape[1]
    row_spec = pl.BlockSpec((tm, D_MODEL), lambda i: (i, 0))
    return pl.pallas_call(
        functools.partial(_mlp_kernel, ff_chunk=1024),
        grid=(t // tm,),
        in_specs=[row_spec, row_spec, _const_spec((D_MODEL, D_MODEL)), _const_spec((1, D_MODEL)),
                  _const_spec((D_MODEL, d_ff)), _const_spec((d_ff, D_MODEL))],
        out_specs=row_spec,
        out_shape=jax.ShapeDtypeStruct((t, D_MODEL), F32),
        compiler_params=_params(("parallel",)),
        name="out_mlp",
    )(x, o, wo, g, wup, wdn)


def _online_step(s, m, l, acc, vt):
    m_new = jnp.maximum(m, jnp.max(s, axis=0, keepdims=True))
    alpha = jnp.exp(m - m_new)
    p = jnp.exp(s - m_new)
    l = alpha * l + jnp.sum(p, axis=0, keepdims=True)
    acc = alpha * acc + _dot(vt, p.astype(BF16))
    return m_new, l, acc


def _first_step(s, vt):
    m = jnp.max(s, axis=0, keepdims=True)
    p = jnp.exp(s - m)
    return m, jnp.sum(p, axis=0, keepdims=True), _dot(vt, p.astype(BF16))


def _causal_mask(tq):
    key = lax.broadcasted_iota(jnp.int32, (tq, tq), 0)
    qry = lax.broadcasted_iota(jnp.int32, (tq, tq), 1)
    return key <= qry


def _moba_prompt_kernel(q_ref, kt_ref, vt_ref, o_ref, k_scr, vt_scr, km_scr, sel_scr, *, n_blk):
    qi = pl.program_id(2)
    tq = q_ref.shape[0]

    @pl.when(qi == 0)
    def _():
        for n in range(n_blk):
            kn = kt_ref[:, n * MOBA_BLOCK:(n + 1) * MOBA_BLOCK].T
            km_scr[n:n + 1, :] = jnp.mean(kn, axis=0, keepdims=True)
            k_scr[n] = kn.astype(BF16)
            vt_scr[n] = vt_ref[:, n * MOBA_BLOCK:(n + 1) * MOBA_BLOCK].astype(BF16)

    q = q_ref[...]
    half = _lane_half((1, LANES))
    blk_id = lax.broadcasted_iota(jnp.int32, (n_blk, 1), 0)
    past = blk_id < qi
    causal = _causal_mask(tq)
    km = km_scr[...]
    kmh, kml = _split_bf16(jnp.concatenate([km, jnp.zeros_like(km)], axis=0))

    qs, carries = [], []
    for a in range(2):
        qa = jnp.where(half == a, q, 0.0)
        qh, ql = _split_bf16(qa)
        gate = (_dot_nt(kmh, qh) + _dot_nt(kmh, ql) + _dot_nt(kml, qh))[:n_blk]
        gate = jnp.where(past, gate, -jnp.inf)
        for n in range(n_blk):
            gn = gate[n:n + 1, :]
            beats = past & ((gate > gn) | ((gate == gn) & (blk_id < n)))
            rank = jnp.sum(beats.astype(F32), axis=0, keepdims=True)
            sel_scr[a, n:n + 1, :] = jnp.where((rank < MOBA_TOPK) & (qi > n), 1.0, 0.0)
        qb = (qa * SCALE).astype(BF16)
        qs.append(qb)
        s = jnp.where(causal, _dot_nt(k_scr[qi], qb), NEG)
        carries.append(_first_step(s, vt_scr[qi, a * HEAD_DIM:(a + 1) * HEAD_DIM, :]))

    def body(n, carry):
        out = []
        for a in range(2):
            m, l, acc = carry[a]
            s = _dot_nt(k_scr[n], qs[a])
            s = jnp.where(sel_scr[a, pl.ds(n, 1), :] > 0.5, s, NEG)
            out.append(_online_step(s, m, l, acc, vt_scr[n, a * HEAD_DIM:(a + 1) * HEAD_DIM, :]))
        return tuple(out)

    carries = lax.fori_loop(0, qi, body, tuple(carries))
    ot = jnp.concatenate([acc / l for (_, l, acc) in carries], axis=0)
    o_ref[...] = ot.T.astype(o_ref.dtype)


def _moba_prompt_attn(q, kt, vt):
    b, _, s = kt.shape
    n_blk = s // MOBA_BLOCK
    tq = MOBA_BLOCK
    n_pairs = D_MODEL // LANES
    q_spec = pl.BlockSpec((tq, LANES), lambda bi, hp, qi: (bi * n_blk + qi, hp))
    kv_spec = pl.BlockSpec((None, LANES, s), lambda bi, hp, qi: (bi, hp, 0))
    return pl.pallas_call(
        functools.partial(_moba_prompt_kernel, n_blk=n_blk),
        grid=(b, n_pairs, n_blk),
        in_specs=[q_spec, kv_spec, kv_spec],
        out_specs=q_spec,
        out_shape=jax.ShapeDtypeStruct((b * s, D_MODEL), BF16),
        scratch_shapes=[pltpu.VMEM((n_blk, MOBA_BLOCK, LANES), BF16),
                        pltpu.VMEM((n_blk, LANES, MOBA_BLOCK), BF16),
                        pltpu.VMEM((n_blk, LANES), F32),
                        pltpu.VMEM((2, n_blk, tq), F32)],
        compiler_params=_params(("parallel", "parallel", "arbitrary")),
        name="moba_prompt_attn",
    )(q, kt, vt)


def _diff_lambda(lam_ref, lam_init):
    lv = lam_ref[...]
    a = jnp.sum(lv[0:1] * lv[1:2], axis=-1, keepdims=True)
    b = jnp.sum(lv[2:3] * lv[3:4], axis=-1, keepdims=True)
    return jnp.exp(a) - jnp.exp(b) + lam_init


def _diff_prompt_kernel(q_ref, k_ref, v_ref, lam_ref, gsub_ref, o_ref, k_scr, vt_scr, *, n_blk, lam_init):
    qi = pl.program_id(2)
    tq = q_ref.shape[0]

    @pl.when(qi == 0)
    def _():
        for n in range(n_blk):
            k_scr[n] = k_ref[n * tq:(n + 1) * tq, :].astype(BF16)
            vt_scr[n] = v_ref[n * tq:(n + 1) * tq, :].T.astype(BF16)

    q = q_ref[...]
    half = _lane_half((1, LANES))
    causal = _causal_mask(tq)
    qs, carries = [], []
    for c in range(2):
        qb = (jnp.where(half == c, q, 0.0) * SCALE).astype(BF16)
        qs.append(qb)
        s = jnp.where(causal, _dot_nt(k_scr[qi], qb), NEG)
        carries.append(_first_step(s, vt_scr[qi]))

    def body(n, carry):
        out = []
        for c in range(2):
            m, l, acc = carry[c]
            out.append(_online_step(_dot_nt(k_scr[n], qs[c]), m, l, acc, vt_scr[n]))
        return tuple(out)

    (_, l1, a1), (_, l2, a2) = lax.fori_loop(0, qi, body, tuple(carries))
    lam = _diff_lambda(lam_ref, lam_init)
    ot = a1 / l1 - lam * (a2 / l2)
    ms = jnp.mean(ot * ot, axis=0, keepdims=True)
    ot = ot * lax.rsqrt(ms + EPS) * gsub_ref[...] * (1.0 - lam_init)
    o = ot.T.astype(o_ref.dtype)
    for c in range(o_ref.shape[0]):
        o_ref[c] = o[c * Q_CHUNK_DIFF:(c + 1) * Q_CHUNK_DIFF]


def _diff_prompt_attn(q, k, v, lam_vec, gsub_col, *, batch, lam_init):
    t = q.shape[0]
    s = t // batch
    tq = 256
    n_blk = s // tq
    n_heads = D_MODEL // LANES
    q_spec = pl.BlockSpec((tq, LANES), lambda bi, h, qi: (bi * n_blk + qi, h))
    kv_spec = pl.BlockSpec((s, LANES), lambda bi, h, qi: (bi, h))
    per_tile = tq // Q_CHUNK_DIFF
    o_spec = pl.BlockSpec((per_tile, None, Q_CHUNK_DIFF, LANES), lambda bi, h, qi: (qi, bi, 0, h))
    o_shape = jax.ShapeDtypeStruct((s // Q_CHUNK_DIFF, batch, Q_CHUNK_DIFF, D_MODEL), BF16)
    return pl.pallas_call(
        functools.partial(_diff_prompt_kernel, n_blk=n_blk, lam_init=lam_init),
        grid=(batch, n_heads, n_blk),
        in_specs=[q_spec, kv_spec, kv_spec, _const_spec((4, HEAD_DIM)), _const_spec((LANES, 1))],
        out_specs=o_spec,
        out_shape=o_shape,
        scratch_shapes=[pltpu.VMEM((n_blk, tq, LANES), BF16),
                        pltpu.VMEM((n_blk, LANES, tq), BF16)],
        compiler_params=_params(("parallel", "parallel", "arbitrary")),
        name="diff_prompt_attn",
    )(q, k, v, lam_vec, gsub_col).reshape(t, D_MODEL)


def _pad_rows16(x):
    return jnp.concatenate([x, jnp.zeros_like(x)], axis=0).astype(BF16)


def _moba_sample_kernel(pt_ref, q_ref, kn_ref, vn_ref, ka_ref, kb_ref, va_ref, vb_ref, o_ref,
                        qp_scr, a_scr, g_scr, m_scr, l_scr, *, n_blk):
    del pt_ref
    j = pl.program_id(1)
    n_tok = q_ref.shape[0]
    n_pairs = D_MODEL // LANES
    rows = 2 * n_tok
    half = _lane_half((1, LANES))

    @pl.when(j == 0)
    def _():
        for c in range(n_pairs):
            x = q_ref[:, c * LANES:(c + 1) * LANES] * SCALE
            qp = jnp.concatenate([jnp.where(half == 0, x, 0.0), jnp.where(half == 1, x, 0.0)], axis=0)
            hi, lo = _split_bf16(qp)
            qp_scr[c] = jnp.concatenate([hi, lo], axis=0)
        g_scr[...] = jnp.zeros_like(g_scr)
        m_scr[...] = jnp.zeros_like(m_scr)
        l_scr[...] = jnp.zeros_like(l_scr)

    s_parts = []
    for c in range(n_pairs):
        kp = jnp.concatenate([ka_ref[c * LANES:(c + 1) * LANES, :], kb_ref[c * LANES:(c + 1) * LANES, :]],
                             axis=1).astype(BF16)
        s2 = _dot(qp_scr[c], kp)
        s_parts.append(s2[:rows] + s2[rows:])
    s = jnp.concatenate(s_parts, axis=0)
    gate = jnp.mean(s, axis=-1, keepdims=True)
    m = jnp.max(s, axis=-1, keepdims=True)
    p = jnp.exp(s - m)
    l = jnp.sum(p, axis=-1, keepdims=True)
    pb = p.astype(BF16)
    a_parts = []
    for c in range(n_pairs):
        vp = jnp.concatenate([va_ref[c * LANES:(c + 1) * LANES, :], vb_ref[c * LANES:(c + 1) * LANES, :]],
                             axis=1).astype(BF16)
        a_parts.append(_dot_nt(pb[c * rows:(c + 1) * rows], vp))
    a_scr[j] = jnp.concatenate(a_parts, axis=0)
    here = lax.broadcasted_iota(jnp.int32, (1, LANES), 1) == j
    g_scr[...] = jnp.where(here, gate, g_scr[...])
    m_scr[...] = jnp.where(here, m, m_scr[...])
    l_scr[...] = jnp.where(here, l, l_scr[...])

    @pl.when(j == n_blk - 1)
    def _():
        lane = lax.broadcasted_iota(jnp.int32, (1, LANES), 1).astype(F32)
        g = jnp.where(lane < n_blk, g_scr[...], -jnp.inf)
        sel = jnp.zeros(g.shape, jnp.bool_)
        for _ in range(MOBA_TOPK):
            mx = jnp.max(g, axis=-1, keepdims=True)
            idx = jnp.min(jnp.where(g == mx, lane, float(LANES)), axis=-1, keepdims=True)
            pick = lane == idx
            sel = sel | pick
            g = jnp.where(pick, -jnp.inf, g)
        kn, vn = _pad_rows16(kn_ref[...]), _pad_rows16(vn_ref[...])
        qrow = lax.broadcasted_iota(jnp.int32, (rows, rows), 0) % n_tok
        kcol = lax.broadcasted_iota(jnp.int32, (rows, rows), 1)
        so = jnp.concatenate(
            [jnp.where(kcol <= qrow, _dot_nt(qp_scr[c, :rows, :], kn[:, c * LANES:(c + 1) * LANES]), NEG)
             for c in range(n_pairs)], axis=0)
        m_own = jnp.max(so, axis=-1, keepdims=True)
        m_sel = jnp.where(sel, m_scr[...], NEG)
        m_all = jnp.maximum(m_own, jnp.max(m_sel, axis=-1, keepdims=True))
        w = jnp.where(sel, jnp.exp(m_sel - m_all), 0.0)
        po = jnp.exp(so - m_all)
        l_all = jnp.sum(w * l_scr[...], axis=-1, keepdims=True) + jnp.sum(po, axis=-1, keepdims=True)
        pob = po.astype(BF16)
        acc = jnp.concatenate(
            [_dot(pob[c * rows:(c + 1) * rows], vn[:, c * LANES:(c + 1) * LANES]) for c in range(n_pairs)],
            axis=0)
        for n in range(n_blk):
            acc = acc + a_scr[n] * w[:, n:n + 1]
        acc = acc / l_all
        for c in range(n_pairs):
            o_ref[:, c * LANES:(c + 1) * LANES] = jnp.where(
                half == 0, acc[c * rows:c * rows + n_tok], acc[c * rows + n_tok:(c + 1) * rows])


def _moba_sample_attn(page_table, q, k_new, v_new, k_cache, v_cache, *, layer):
    n_seq, n_pages = page_table.shape
    n_tok = q.shape[0] // n_seq
    page = k_cache.shape[-1]
    per_blk = MOBA_BLOCK // page
    assert per_blk == 2 and n_tok * (D_MODEL // HEAD_DIM) == LANES
    n_blk = n_pages // per_blk
    row_spec = pl.BlockSpec((n_tok, D_MODEL), lambda b, j, pt: (b, 0))

    def page_spec(i):
        return pl.BlockSpec((None, None, D_MODEL, page), lambda b, j, pt: (layer, pt[b, per_blk * j + i], 0, 0))

    grid_spec = pltpu.PrefetchScalarGridSpec(
        num_scalar_prefetch=1,
        grid=(n_seq, n_blk),
        in_specs=[row_spec, row_spec, row_spec, page_spec(0), page_spec(1), page_spec(0), page_spec(1)],
        out_specs=row_spec,
        scratch_shapes=[pltpu.VMEM((D_MODEL // LANES, 4 * n_tok, LANES), BF16),
                        pltpu.VMEM((n_blk, LANES, LANES), F32),
                        pltpu.VMEM((LANES, LANES), F32),
                        pltpu.VMEM((LANES, LANES), F32),
                        pltpu.VMEM((LANES, LANES), F32)])
    return pl.pallas_call(
        functools.partial(_moba_sample_kernel, n_blk=n_blk),
        grid_spec=grid_spec,
        out_shape=jax.ShapeDtypeStruct(q.shape, F32),
        compiler_params=_params(("parallel", "arbitrary")),
        name="moba_sample_attn",
    )(page_table, q, k_new, v_new, k_cache, k_cache, v_cache, v_cache)


def _diff_sample_kernel(pt_ref, q_ref, kn_ref, vn_ref, lam_ref, gsub_ref, *refs,
                        pages_per_step, n_steps, lam_init):
    del pt_ref
    k_refs = refs[:pages_per_step]
    v_refs = refs[pages_per_step:2 * pages_per_step]
    o_ref, qd_scr, bias_scr, m_scr, l_scr, acc_scr = refs[2 * pages_per_step:]
    j = pl.program_id(1)
    n_tok = q_ref.shape[0]
    n_heads = D_MODEL // LANES
    rows = 2 * n_tok
    half = _lane_half((1, LANES))

    @pl.when(j == 0)
    def _():
        for h in range(n_heads):
            x = q_ref[:, h * LANES:(h + 1) * LANES] * SCALE
            qd_scr[h * rows:(h + 1) * rows, :] = jnp.concatenate(
                [jnp.where(half == 0, x, 0.0), jnp.where(half == 1, x, 0.0)], axis=0).astype(BF16)
        r = lax.broadcasted_iota(jnp.int32, bias_scr.shape, 0) // rows
        c = lax.broadcasted_iota(jnp.int32, bias_scr.shape, 1) % n_heads
        bias_scr[...] = jnp.where(r == c, 0.0, NEG)
        kn, vn = _pad_rows16(kn_ref[...]), _pad_rows16(vn_ref[...])
        qrow = lax.broadcasted_iota(jnp.int32, (rows, rows), 0) % n_tok
        kcol = lax.broadcasted_iota(jnp.int32, (rows, rows), 1)
        so = jnp.concatenate(
            [jnp.where(kcol <= qrow,
                       _dot_nt(qd_scr[h * rows:(h + 1) * rows, :], kn[:, h * LANES:(h + 1) * LANES]), NEG)
             for h in range(n_heads)], axis=0)
        m = jnp.max(so, axis=-1, keepdims=True)
        po = jnp.exp(so - m)
        pob = po.astype(BF16)
        m_scr[...] = m
        l_scr[...] = jnp.sum(po, axis=-1, keepdims=True)
        acc_scr[...] = jnp.concatenate(
            [_dot(pob[h * rows:(h + 1) * rows], vn[:, h * LANES:(h + 1) * LANES]) for h in range(n_heads)],
            axis=0)

    qd = qd_scr[...]
    m, l, acc = m_scr[...], l_scr[...], acc_scr[...]
    for i in range(pages_per_step):
        s = _dot_nt(qd, k_refs[i][...].astype(BF16)) + bias_scr[...]
        m_new = jnp.maximum(m, jnp.max(s, axis=-1, keepdims=True))
        alpha = jnp.exp(m - m_new)
        p = jnp.exp(s - m_new)
        l = alpha * l + jnp.sum(p, axis=-1, keepdims=True)
        acc = alpha * acc + _dot(p.astype(BF16), v_refs[i][...].astype(BF16))
        m = m_new
    m_scr[...] = m
    l_scr[...] = l
    acc_scr[...] = acc

    @pl.when(j == n_steps - 1)
    def _():
        lam = _diff_lambda(lam_ref, lam_init)
        on = acc / l
        for h in range(n_heads):
            o = on[h * rows:h * rows + n_tok] - lam * on[h * rows + n_tok:(h + 1) * rows]
            ms = jnp.mean(o * o, axis=-1, keepdims=True)
            o_ref[:, h * LANES:(h + 1) * LANES] = o * lax.rsqrt(ms + EPS) * gsub_ref[...] * (1.0 - lam_init)


def _diff_sample_attn(page_table, q, k_new, v_new, lam_vec, gsub_row, k_cache, v_cache, *, layer, lam_init):
    n_seq, n_pages = page_table.shape
    n_tok = q.shape[0] // n_seq
    page_rows = k_cache.shape[2]
    pages_per_step = 4
    n_steps = n_pages // pages_per_step
    assert 2 * n_tok * (D_MODEL // LANES) == LANES
    row_spec = pl.BlockSpec((n_tok, D_MODEL), lambda b, j, pt: (b, 0))

    def page_spec(i):
        return pl.BlockSpec((None, None, page_rows, LANES),
                            lambda b, j, pt: (layer, pt[b, pages_per_step * j + i], 0, 0))

    def const(shape):
        zeros = (0,) * len(shape)
        return pl.BlockSpec(shape, lambda b, j, pt: zeros)

    pages = [page_spec(i) for i in range(pages_per_step)]
    grid_spec = pltpu.PrefetchScalarGridSpec(
        num_scalar_prefetch=1,
        grid=(n_seq, n_steps),
        in_specs=[row_spec, row_spec, row_spec, const((4, HEAD_DIM)), const((1, LANES))] + pages + pages,
        out_specs=row_spec,
        scratch_shapes=[pltpu.VMEM((LANES, LANES), BF16),
                        pltpu.VMEM((LANES, page_rows), F32),
                        pltpu.VMEM((LANES, 1), F32),
                        pltpu.VMEM((LANES, 1), F32),
                        pltpu.VMEM((LANES, LANES), F32)])
    return pl.pallas_call(
        functools.partial(_diff_sample_kernel, pages_per_step=pages_per_step, n_steps=n_steps,
                          lam_init=lam_init),
        grid_spec=grid_spec,
        out_shape=jax.ShapeDtypeStruct(q.shape, F32),
        compiler_params=_params(("parallel", "arbitrary")),
        name="diff_sample_attn",
    )(page_table, q, k_new, v_new, lam_vec, gsub_row,
      *([k_cache] * pages_per_step), *([v_cache] * pages_per_step))


def kernel(x_prompt, x_sample, cache_moba_k, cache_moba_v, cache_diff_k, cache_diff_v, page_table, norm_mix, norm_mlp, moba_w_qkv, moba_w_o, moba_q_norm, moba_k_norm, diff_w_qkv, diff_w_o, diff_q_norm, diff_k_norm, diff_lambda, diff_subln, mlp_w_up, mlp_w_down):
    batch, seq, d = x_prompt.shape
    n_seq, n_tok, _ = x_sample.shape
    depth = norm_mix.shape[0]
    n_lay, n_phys, page, a_heads, _ = cache_moba_k.shape
    b_heads = cache_diff_k.shape[3]
    past_len = page_table.shape[1] * page
    tm = 256

    rope_p = _rope_tables(jnp.arange(seq))
    rope_s = _rope_tables(past_len + jnp.arange(n_seq * n_tok) % n_tok)

    moba_kc = jnp.transpose(cache_moba_k, (0, 1, 3, 4, 2)).reshape(n_lay, n_phys, d, page)
    moba_vc = jnp.transpose(cache_moba_v, (0, 1, 3, 4, 2)).reshape(n_lay, n_phys, d, page)
    diff_kc = cache_diff_k.reshape(cache_diff_k.shape[0], n_phys, page * b_heads, 2 * HEAD_DIM)
    diff_vc = cache_diff_v.reshape(cache_diff_v.shape[0], n_phys, page * b_heads, 2 * HEAD_DIM)

    def two_heads(gain):
        return jnp.tile(gain, LANES // HEAD_DIM).reshape(1, LANES)

    xp = x_prompt.reshape(batch * seq, d)
    xs = x_sample.reshape(n_seq * n_tok, d)
    mk_p, mv_p, mk_s, mv_s = [], [], [], []
    dk_p, dv_p, dk_s, dv_s = [], [], [], []
    for i in range(depth):
        j = i // N_MIXERS
        moba = i % N_MIXERS == 0
        w_qkv = (moba_w_qkv if moba else diff_w_qkv)[j].astype(BF16)
        w_o = (moba_w_o if moba else diff_w_o)[j].astype(BF16)
        gq = two_heads((moba_q_norm if moba else diff_q_norm)[j])
        gk = two_heads((moba_k_norm if moba else diff_k_norm)[j])
        g_mix = norm_mix[i].reshape(1, d)
        qp, kp, vp = _qkv_proj(xp, g_mix, w_qkv, gq, gk, rope_p, seq_tiles=seq // tm, transpose_kv=moba, tm=tm)
        qs, ks, vs = _qkv_proj(xs, g_mix, w_qkv, gq, gk, rope_s, seq_tiles=1, transpose_kv=False,
                               tm=n_seq * n_tok)
        if moba:
            op = _moba_prompt_attn(qp, kp, vp)
            os_ = _moba_sample_attn(page_table, qs, ks, vs, moba_kc, moba_vc, layer=j)
            mk_p.append(kp); mv_p.append(vp); mk_s.append(ks); mv_s.append(vs)
        else:
            lam_init = 0.8 - 0.6 * math.exp(-0.3 * i)
            op = _diff_prompt_attn(qp, kp, vp, diff_lambda[j], diff_subln[j].reshape(LANES, 1),
                                   batch=batch, lam_init=lam_init)
            os_ = _diff_sample_attn(page_table, qs, ks, vs, diff_lambda[j], diff_subln[j].reshape(1, LANES),
                                    diff_kc, diff_vc, layer=j, lam_init=lam_init)
            dk_p.append(kp); dv_p.append(vp); dk_s.append(ks); dv_s.append(vs)
        g_mlp = norm_mlp[i].reshape(1, d)
        w_up, w_dn = mlp_w_up[i].astype(BF16), mlp_w_down[i].astype(BF16)
        xp = _out_mlp(xp, op, w_o, g_mlp, w_up, w_dn, tm=tm)
        xs = _out_mlp(xs, os_, w_o, g_mlp, w_up, w_dn, tm=n_seq * n_tok)

    def moba_prompt_out(parts):
        return jnp.transpose(jnp.stack(parts).reshape(len(parts), batch, a_heads, HEAD_DIM, seq), (0, 1, 4, 2, 3))

    def rows_out(parts, n_b, n_s, heads):
        return jnp.stack(parts).reshape(len(parts), n_b, n_s, heads, d // heads)

    return (xp.reshape(batch, seq, d), xs.reshape(n_seq, n_tok, d),
            moba_prompt_out(mk_p), moba_prompt_out(mv_p),
            rows_out(dk_p, batch, seq, b_heads), rows_out(dv_p, batch, seq, b_heads),
            rows_out(mk_s, n_seq, n_tok, a_heads), rows_out(mv_s, n_seq, n_tok, a_heads),
            rows_out(dk_s, n_seq, n_tok, b_heads), rows_out(dv_s, n_seq, n_tok, b_heads))
```

```python
import functools
import math

import jax
import jax.numpy as jnp
from jax import lax
from jax.experimental import pallas as pl
from jax.experimental.pallas import tpu as pltpu

F32 = jnp.float32
BF16 = jnp.bfloat16

D_MODEL = 1024
HEAD_DIM = 64
ROT_DIM = HEAD_DIM // 4
ROPE_THETA = 500000.0
MOBA_BLOCK = 256
MOBA_TOPK = 3
Q_CHUNK_DIFF = 128
EPS = 1e-6
N_MIXERS = 2
LANES = 128
BF16_ROWS = 16
SCALE = HEAD_DIM ** -0.5
NEG = -1e30
VMEM_LIMIT = 56 * 1024 * 1024

_NT = (((1,), (1,)), ((), ()))


def _dot(a, b):
    return jnp.dot(a, b, preferred_element_type=F32)


def _dot_nt(a, b):
    return lax.dot_general(a, b, _NT, preferred_element_type=F32)


def _split_bf16(x):
    hi = x.astype(BF16)
    lo = (x - hi.astype(F32)).astype(BF16)
    return hi, lo


def _params(sem):
    return pltpu.CompilerParams(dimension_semantics=sem, vmem_limit_bytes=VMEM_LIMIT)


def _const_spec(shape):
    zeros = (0,) * len(shape)
    return pl.BlockSpec(shape, lambda *_: zeros, pipeline_mode=pl.Buffered(1))


def _lane_half(shape):
    return lax.broadcasted_iota(jnp.int32, shape, len(shape) - 1) // HEAD_DIM


def _qkv_kernel(x_ref, g_ref, w_ref, gq_ref, gk_ref, cos_ref, sa_ref, sb_ref,
                q_ref, k_ref, v_ref, *, transpose_kv):
    x = x_ref[...]
    ms = jnp.mean(x * x, axis=-1, keepdims=True)
    h = (x * lax.rsqrt(ms + EPS) * g_ref[...]).astype(BF16)
    y = _dot(h, w_ref[...])
    cos, sa, sb = cos_ref[...], sa_ref[...], sb_ref[...]
    first = _lane_half((1, LANES)) == 0

    def norm_rope(blk, gain):
        sq = blk * blk
        s0 = jnp.sum(jnp.where(first, sq, 0.0), axis=-1, keepdims=True)
        s1 = jnp.sum(jnp.where(first, 0.0, sq), axis=-1, keepdims=True)
        r = lax.rsqrt(jnp.where(first, s0, s1) * (1.0 / HEAD_DIM) + EPS)
        n = blk * r * gain
        return n * cos + pltpu.roll(n, LANES - ROT_DIM // 2, 1) * sa + pltpu.roll(n, ROT_DIM // 2, 1) * sb

    n_tiles = D_MODEL // LANES
    for c in range(n_tiles):
        sl = slice(c * LANES, (c + 1) * LANES)
        q_ref[:, sl] = norm_rope(y[:, sl], gq_ref[...])
        kb = norm_rope(y[:, D_MODEL + c * LANES:D_MODEL + (c + 1) * LANES], gk_ref[...])
        vb = y[:, 2 * D_MODEL + c * LANES:2 * D_MODEL + (c + 1) * LANES]
        if transpose_kv:
            k_ref[sl, :] = kb.T
            v_ref[sl, :] = vb.T
        else:
            k_ref[:, sl] = kb
            v_ref[:, sl] = vb


def _qkv_proj(x, g, w, gq, gk, rope, *, seq_tiles, transpose_kv, tm):
    t = x.shape[0]
    n_tiles = t // tm
    cos, sa, sb = rope
    row_spec = pl.BlockSpec((tm, D_MODEL), lambda i: (i, 0))
    rope_spec = pl.BlockSpec((tm, LANES), lambda i: (i % seq_tiles, 0))
    if transpose_kv:
        kv_shape = jax.ShapeDtypeStruct((n_tiles // seq_tiles, D_MODEL, seq_tiles * tm), F32)
        kv_spec = pl.BlockSpec((None, D_MODEL, tm), lambda i: (i // seq_tiles, 0, i % seq_tiles))
    else:
        kv_shape = jax.ShapeDtypeStruct((t, D_MODEL), F32)
        kv_spec = row_spec
    return pl.pallas_call(
        functools.partial(_qkv_kernel, transpose_kv=transpose_kv),
        grid=(n_tiles,),
        in_specs=[row_spec, _const_spec((1, D_MODEL)), _const_spec((D_MODEL, 3 * D_MODEL)),
                  _const_spec((1, LANES)), _const_spec((1, LANES)), rope_spec, rope_spec, rope_spec],
        out_specs=[row_spec, kv_spec, kv_spec],
        out_shape=[jax.ShapeDtypeStruct((t, D_MODEL), F32), kv_shape, kv_shape],
        compiler_params=_params(("parallel",)),
        name="qkv_proj_t" if transpose_kv else "qkv_proj",
    )(x, g, w, gq, gk, cos, sa, sb)


def _rope_tables(pos):
    half = ROT_DIM // 2
    inv_freq = ROPE_THETA ** (-jnp.arange(half, dtype=F32) * (2.0 / ROT_DIM))
    ang = pos.astype(F32)[:, None] * inv_freq[None, :]
    lane = jnp.arange(LANES)
    d = lane % HEAD_DIM
    cos_l = jnp.cos(ang)[:, lane % half]
    sin_l = jnp.sin(ang)[:, lane % half]
    cos = jnp.where(d < ROT_DIM, cos_l, 1.0)
    sa = jnp.where(d < half, -sin_l, 0.0)
    sb = jnp.where((d >= half) & (d < ROT_DIM), sin_l, 0.0)
    return cos.astype(F32), sa.astype(F32), sb.astype(F32)


def _mlp_kernel(x_ref, o_ref, wo_ref, g_ref, wup_ref, wdn_ref, out_ref, *, ff_chunk):
    x1 = x_ref[...] + _dot(o_ref[...].astype(BF16), wo_ref[...])
    ms = jnp.mean(x1 * x1, axis=-1, keepdims=True)
    h = (x1 * lax.rsqrt(ms + EPS) * g_ref[...]).astype(BF16)
    acc = x1
    for c in range(wup_ref.shape[1] // ff_chunk):
        u = jnp.maximum(_dot(h, wup_ref[:, c * ff_chunk:(c + 1) * ff_chunk]), 0.0)
        acc = acc + _dot((u * u).astype(BF16), wdn_ref[c * ff_chunk:(c + 1) * ff_chunk, :])
    out_ref[...] = acc


def _out_mlp(x, o, wo, g, wup, wdn, *, tm):
    t = x.shape[0]
    d_ff = wup.shape[1]
    row_spec = pl.BlockSpec((tm, D_MODEL), lambda i: (i, 0))
    return pl.pallas_call(
        functools.partial(_mlp_kernel, ff_chunk=1024),
        grid=(t // tm,),
        in_specs=[row_spec, row_spec, _const_spec((D_MODEL, D_MODEL)), _const_spec((1, D_MODEL)),
                  _const_spec((D_MODEL, d_ff)), _const_spec((d_ff, D_MODEL))],
        out_specs=row_spec,
        out_shape=jax.ShapeDtypeStruct((t, D_MODEL), F32),
        compiler_params=_params(("parallel",)),
        name="out_mlp",
    )(x, o, wo, g, wup, wdn)


def _causal_mask(tq):
    key = lax.broadcasted_iota(jnp.int32, (tq, tq), 0)
    qry = lax.broadcasted_iota(jnp.int32, (tq, tq), 1)
    return key <= qry


def _softmax_two_pass(k_scr, qb, s_scr, p_scr, n_blocks, bias):
    blk = k_scr.shape[1]
    causal = _causal_mask(blk)
    m = None
    for n in range(n_blocks):
        s = _dot_nt(k_scr[n], qb)
        if n == n_blocks - 1:
            s = jnp.where(causal, s, NEG)
        elif bias is not None:
            s = s + bias[n]
        s_scr[n] = s
        cm = jnp.max(s, axis=0, keepdims=True)
        m = cm if m is None else jnp.maximum(m, cm)
    l = None
    for n in range(n_blocks):
        p = jnp.exp(s_scr[n] - m)
        cs = jnp.sum(p, axis=0, keepdims=True)
        l = cs if l is None else l + cs
        p_scr[n * blk:(n + 1) * blk, :] = p.astype(BF16)
    return l


def _moba_bias(km_scr, qa, n_past):
    km = km_scr[...]
    n_blk = km.shape[0]
    pad = -n_blk % BF16_ROWS
    kmh, kml = _split_bf16(jnp.concatenate([km, jnp.zeros((pad, km.shape[1]), F32)], axis=0))
    qh, ql = _split_bf16(qa)
    gate = (_dot_nt(kmh, qh) + _dot_nt(kmh, ql) + _dot_nt(kml, qh))[:n_blk]
    blk_id = lax.broadcasted_iota(jnp.int32, (n_blk, 1), 0)
    past = blk_id < n_past
    gate = jnp.where(past, gate, -jnp.inf)
    bias = []
    for n in range(n_past):
        gn = gate[n:n + 1, :]
        beats = past & ((gate > gn) | ((gate == gn) & (blk_id < n)))
        rank = jnp.sum(beats.astype(F32), axis=0, keepdims=True)
        bias.append(jnp.where(rank < MOBA_TOPK, 0.0, NEG))
    return bias


def _moba_prompt_kernel(q_ref, kt_ref, vt_ref, o_ref, k_scr, vt_scr, km_scr, s_scr, p_scr, *, n_blk):
    qi = pl.program_id(2)
    blk = MOBA_BLOCK

    @pl.when(qi == 0)
    def _():
        for n in range(n_blk):
            kn = kt_ref[:, n * blk:(n + 1) * blk].T
            km_scr[n:n + 1, :] = jnp.mean(kn, axis=0, keepdims=True)
            k_scr[n] = kn.astype(BF16)
        vt_scr[...] = vt_ref[...].astype(BF16)

    def tile(k):
        q = q_ref[...]
        half = _lane_half((1, LANES))
        n_keys = (k + 1) * blk
        outs = []
        for a in range(2):
            qa = jnp.where(half == a, q, 0.0)
            bias = _moba_bias(km_scr, qa, k) if k > MOBA_TOPK else None
            qb = (qa * SCALE).astype(BF16)
            l = _softmax_two_pass(k_scr, qb, s_scr.at[a], p_scr.at[a], k + 1, bias)
            acc = _dot(vt_scr[a * HEAD_DIM:(a + 1) * HEAD_DIM, :n_keys], p_scr[a, :n_keys, :])
            outs.append(acc / l)
        o_ref[...] = jnp.concatenate(outs, axis=0).T.astype(o_ref.dtype)

    for k in range(n_blk):
        pl.when(qi == k)(functools.partial(tile, k))


def _moba_prompt_attn(q, kt, vt):
    b, _, s = kt.shape
    n_blk = s // MOBA_BLOCK
    tq = MOBA_BLOCK
    n_pairs = D_MODEL // LANES
    q_spec = pl.BlockSpec((tq, LANES), lambda bi, hp, qi: (bi * n_blk + qi, hp))
    kv_spec = pl.BlockSpec((None, LANES, s), lambda bi, hp, qi: (bi, hp, 0))
    return pl.pallas_call(
        functools.partial(_moba_prompt_kernel, n_blk=n_blk),
        grid=(b, n_pairs, n_blk),
        in_specs=[q_spec, kv_spec, kv_spec],
        out_specs=q_spec,
        out_shape=jax.ShapeDtypeStruct((b * s, D_MODEL), BF16),
        scratch_shapes=[pltpu.VMEM((n_blk, MOBA_BLOCK, LANES), BF16),
                        pltpu.VMEM((LANES, s), BF16),
                        pltpu.VMEM((n_blk, LANES), F32),
                        pltpu.VMEM((2, n_blk, MOBA_BLOCK, tq), F32),
                        pltpu.VMEM((2, s, tq), BF16)],
        compiler_params=_params(("parallel", "parallel", "arbitrary")),
        name="moba_prompt_attn",
    )(q, kt, vt)


def _diff_lambda(lam_ref, lam_init):
    lv = lam_ref[...]
    a = jnp.sum(lv[0:1] * lv[1:2], axis=-1, keepdims=True)
    b = jnp.sum(lv[2:3] * lv[3:4], axis=-1, keepdims=True)
    return jnp.exp(a) - jnp.exp(b) + lam_init


def _diff_prompt_kernel(q_ref, k_ref, v_ref, lam_ref, gsub_ref, o_ref, k_scr, vt_scr, s_scr, p_scr,
                        *, n_blk, lam_init):
    qi = pl.program_id(2)
    tq = q_ref.shape[0]

    @pl.when(qi == 0)
    def _():
        for n in range(n_blk):
            k_scr[n] = k_ref[n * tq:(n + 1) * tq, :].astype(BF16)
            vt_scr[:, n * tq:(n + 1) * tq] = v_ref[n * tq:(n + 1) * tq, :].T.astype(BF16)

    def tile(k):
        q = q_ref[...]
        half = _lane_half((1, LANES))
        n_keys = (k + 1) * tq
        outs = []
        for c in range(2):
            qb = (jnp.where(half == c, q, 0.0) * SCALE).astype(BF16)
            l = _softmax_two_pass(k_scr, qb, s_scr.at[c], p_scr.at[c], k + 1, None)
            outs.append(_dot(vt_scr[:, :n_keys], p_scr[c, :n_keys, :]) / l)
        lam = _diff_lambda(lam_ref, lam_init)
        ot = outs[0] - lam * outs[1]
        ms = jnp.mean(ot * ot, axis=0, keepdims=True)
        ot = ot * lax.rsqrt(ms + EPS) * gsub_ref[...] * (1.0 - lam_init)
        o = ot.T.astype(o_ref.dtype)
        for c in range(o_ref.shape[0]):
            o_ref[c] = o[c * Q_CHUNK_DIFF:(c + 1) * Q_CHUNK_DIFF]

    for k in range(n_blk):
        pl.when(qi == k)(functools.partial(tile, k))


def _diff_prompt_attn(q, k, v, lam_vec, gsub_col, *, batch, lam_init):
    t = q.shape[0]
    s = t // batch
    tq = 256
    assert s % tq == 0 and tq % Q_CHUNK_DIFF == 0
    n_blk = s // tq
    n_heads = D_MODEL // LANES
    q_spec = pl.BlockSpec((tq, LANES), lambda bi, h, qi: (bi * n_blk + qi, h))
    kv_spec = pl.BlockSpec((s, LANES), lambda bi, h, qi: (bi, h))
    per_tile = tq // Q_CHUNK_DIFF
    o_spec = pl.BlockSpec((per_tile, None, Q_CHUNK_DIFF, LANES), lambda bi, h, qi: (qi, bi, 0, h))
    o_shape = jax.ShapeDtypeStruct((s // Q_CHUNK_DIFF, batch, Q_CHUNK_DIFF, D_MODEL), BF16)
    return pl.pallas_call(
        functools.partial(_diff_prompt_kernel, n_blk=n_blk, lam_init=lam_init),
        grid=(batch, n_heads, n_blk),
        in_specs=[q_spec, kv_spec, kv_spec, _const_spec((4, HEAD_DIM)), _const_spec((LANES, 1))],
        out_specs=o_spec,
        out_shape=o_shape,
        scratch_shapes=[pltpu.VMEM((n_blk, tq, LANES), BF16),
                        pltpu.VMEM((LANES, s), BF16),
                        pltpu.VMEM((2, n_blk, tq, tq), F32),
                        pltpu.VMEM((2, s, tq), BF16)],
        compiler_params=_params(("parallel", "parallel", "arbitrary")),
        name="diff_prompt_attn",
    )(q, k, v, lam_vec, gsub_col).reshape(t, D_MODEL)


def _pad_rows16(x):
    return jnp.concatenate([x, jnp.zeros_like(x)], axis=0).astype(BF16)


def _moba_sample_kernel(pt_ref, q_ref, kn_ref, vn_ref, *refs, n_blk, blocks_per_step, pages_per_block):
    del pt_ref
    n_pages = blocks_per_step * pages_per_block
    k_refs, v_refs = refs[:n_pages], refs[n_pages:2 * n_pages]
    o_ref, qp_scr, a_scr, g_scr, m_scr, l_scr = refs[2 * n_pages:]
    j = pl.program_id(1)
    n_tok = q_ref.shape[0]
    n_pairs = D_MODEL // LANES
    rows = 2 * n_tok
    half = _lane_half((1, LANES))

    @pl.when(j == 0)
    def _():
        for c in range(n_pairs):
            x = q_ref[:, c * LANES:(c + 1) * LANES] * SCALE
            qp = jnp.concatenate([jnp.where(half == 0, x, 0.0), jnp.where(half == 1, x, 0.0)], axis=0)
            hi, lo = _split_bf16(qp)
            qp_scr[c] = jnp.concatenate([hi, lo], axis=0)
        g_scr[...] = jnp.zeros_like(g_scr)
        m_scr[...] = jnp.zeros_like(m_scr)
        l_scr[...] = jnp.zeros_like(l_scr)

    lane_id = lax.broadcasted_iota(jnp.int32, (1, LANES), 1)
    for t in range(blocks_per_step):
        pages = range(t * pages_per_block, (t + 1) * pages_per_block)
        s_parts = []
        for c in range(n_pairs):
            kp = jnp.concatenate([k_refs[i][c * LANES:(c + 1) * LANES, :] for i in pages],
                                 axis=1).astype(BF16)
            s2 = _dot(qp_scr[c], kp)
            s_parts.append(s2[:rows] + s2[rows:])
        s = jnp.concatenate(s_parts, axis=0)
        gate = jnp.mean(s, axis=-1, keepdims=True)
        m = jnp.max(s, axis=-1, keepdims=True)
        p = jnp.exp(s - m)
        l = jnp.sum(p, axis=-1, keepdims=True)
        pb = p.astype(BF16)
        a_parts = []
        for c in range(n_pairs):
            vp = jnp.concatenate([v_refs[i][c * LANES:(c + 1) * LANES, :] for i in pages],
                                 axis=1).astype(BF16)
            a_parts.append(_dot_nt(pb[c * rows:(c + 1) * rows], vp))
        blk = j * blocks_per_step + t
        a_scr[blk] = jnp.concatenate(a_parts, axis=0)
        here = lane_id == blk
        g_scr[...] = jnp.where(here, gate, g_scr[...])
        m_scr[...] = jnp.where(here, m, m_scr[...])
        l_scr[...] = jnp.where(here, l, l_scr[...])

    @pl.when(j == n_blk // blocks_per_step - 1)
    def _():
        lane = lane_id.astype(F32)
        g = jnp.where(lane < n_blk, g_scr[...], -jnp.inf)
        sel = jnp.zeros(g.shape, jnp.bool_)
        for _ in range(MOBA_TOPK):
            mx = jnp.max(g, axis=-1, keepdims=True)
            idx = jnp.min(jnp.where(g == mx, lane, float(LANES)), axis=-1, keepdims=True)
            pick = lane == idx
            sel = sel | pick
            g = jnp.where(pick, -jnp.inf, g)
        kn, vn = _pad_rows16(kn_ref[...]), _pad_rows16(vn_ref[...])
        qrow = lax.broadcasted_iota(jnp.int32, (rows, rows), 0) % n_tok
        kcol = lax.broadcasted_iota(jnp.int32, (rows, rows), 1)
        so = jnp.concatenate(
            [jnp.where(kcol <= qrow, _dot_nt(qp_scr[c, :rows, :], kn[:, c * LANES:(c + 1) * LANES]), NEG)
             for c in range(n_pairs)], axis=0)
        m_own = jnp.max(so, axis=-1, keepdims=True)
        m_sel = jnp.where(sel, m_scr[...], NEG)
        m_all = jnp.maximum(m_own, jnp.max(m_sel, axis=-1, keepdims=True))
        w = jnp.where(sel, jnp.exp(m_sel - m_all), 0.0)
        po = jnp.exp(so - m_all)
        l_all = jnp.sum(w * l_scr[...], axis=-1, keepdims=True) + jnp.sum(po, axis=-1, keepdims=True)
        pob = po.astype(BF16)
        acc = jnp.concatenate(
            [_dot(pob[c * rows:(c + 1) * rows], vn[:, c * LANES:(c + 1) * LANES]) for c in range(n_pairs)],
            axis=0)
        for n in range(n_blk):
            acc = acc + a_scr[n] * w[:, n:n + 1]
        acc = acc / l_all
        for c in range(n_pairs):
            o_ref[:, c * LANES:(c + 1) * LANES] = jnp.where(
                half == 0, acc[c * rows:c * rows + n_tok], acc[c * rows + n_tok:(c + 1) * rows])


def _moba_sample_attn(page_table, q, k_new, v_new, k_cache, v_cache, *, layer):
    n_seq, n_pages = page_table.shape
    n_tok = q.shape[0] // n_seq
    page = k_cache.shape[-1]
    per_blk = MOBA_BLOCK // page
    n_blk = n_pages // per_blk
    blocks_per_step = 4
    assert n_tok * (D_MODEL // HEAD_DIM) == LANES and n_blk <= LANES and n_blk % blocks_per_step == 0
    per_step = per_blk * blocks_per_step
    row_spec = pl.BlockSpec((n_tok, D_MODEL), lambda b, j, pt: (b, 0))

    def page_spec(i):
        return pl.BlockSpec((None, None, D_MODEL, page), lambda b, j, pt: (layer, pt[b, per_step * j + i], 0, 0))

    pages = [page_spec(i) for i in range(per_step)]
    grid_spec = pltpu.PrefetchScalarGridSpec(
        num_scalar_prefetch=1,
        grid=(n_seq, n_blk // blocks_per_step),
        in_specs=[row_spec, row_spec, row_spec] + pages + pages,
        out_specs=row_spec,
        scratch_shapes=[pltpu.VMEM((D_MODEL // LANES, 4 * n_tok, LANES), BF16),
                        pltpu.VMEM((n_blk, LANES, LANES), F32),
                        pltpu.VMEM((LANES, LANES), F32),
                        pltpu.VMEM((LANES, LANES), F32),
                        pltpu.VMEM((LANES, LANES), F32)])
    return pl.pallas_call(
        functools.partial(_moba_sample_kernel, n_blk=n_blk, blocks_per_step=blocks_per_step,
                          pages_per_block=per_blk),
        grid_spec=grid_spec,
        out_shape=jax.ShapeDtypeStruct(q.shape, F32),
        compiler_params=_params(("parallel", "arbitrary")),
        name="moba_sample_attn",
    )(page_table, q, k_new, v_new, *([k_cache] * per_step), *([v_cache] * per_step))


def _diff_sample_kernel(pt_ref, q_ref, kn_ref, vn_ref, lam_ref, gsub_ref, *refs,
                        pages_per_step, n_steps, lam_init):
    del pt_ref
    k_refs = refs[:pages_per_step]
    v_refs = refs[pages_per_step:2 * pages_per_step]
    o_ref, qd_scr, m_scr, l_scr, acc_scr = refs[2 * pages_per_step:]
    j = pl.program_id(1)
    n_tok = q_ref.shape[0]
    n_heads = D_MODEL // LANES
    page = k_refs[0].shape[0] // n_heads
    rows = 2 * n_tok
    half = _lane_half((1, LANES))

    def head_rows(page_refs, h):
        return jnp.concatenate([r[pl.ds(h, page, stride=n_heads), :] for r in page_refs], axis=0).astype(BF16)

    @pl.when(j == 0)
    def _():
        for h in range(n_heads):
            x = q_ref[:, h * LANES:(h + 1) * LANES] * SCALE
            qd_scr[h * rows:(h + 1) * rows, :] = jnp.concatenate(
                [jnp.where(half == 0, x, 0.0), jnp.where(half == 1, x, 0.0)], axis=0).astype(BF16)
        kn, vn = _pad_rows16(kn_ref[...]), _pad_rows16(vn_ref[...])
        qrow = lax.broadcasted_iota(jnp.int32, (rows, rows), 0) % n_tok
        kcol = lax.broadcasted_iota(jnp.int32, (rows, rows), 1)
        so = jnp.concatenate(
            [jnp.where(kcol <= qrow,
                       _dot_nt(qd_scr[h * rows:(h + 1) * rows, :], kn[:, h * LANES:(h + 1) * LANES]), NEG)
             for h in range(n_heads)], axis=0)
        m = jnp.max(so, axis=-1, keepdims=True)
        po = jnp.exp(so - m)
        pob = po.astype(BF16)
        m_scr[...] = m
        l_scr[...] = jnp.sum(po, axis=-1, keepdims=True)
        acc_scr[...] = jnp.concatenate(
            [_dot(pob[h * rows:(h + 1) * rows], vn[:, h * LANES:(h + 1) * LANES]) for h in range(n_heads)],
            axis=0)

    s = jnp.concatenate([_dot_nt(qd_scr[h * rows:(h + 1) * rows, :], head_rows(k_refs, h))
                         for h in range(n_heads)], axis=0)
    m = m_scr[...]
    m_new = jnp.maximum(m, jnp.max(s, axis=-1, keepdims=True))
    alpha = jnp.exp(m - m_new)
    p = jnp.exp(s - m_new)
    pb = p.astype(BF16)
    pv = jnp.concatenate([_dot(pb[h * rows:(h + 1) * rows], head_rows(v_refs, h)) for h in range(n_heads)],
                         axis=0)
    m_scr[...] = m_new
    l = alpha * l_scr[...] + jnp.sum(p, axis=-1, keepdims=True)
    acc = alpha * acc_scr[...] + pv
    l_scr[...] = l
    acc_scr[...] = acc

    @pl.when(j == n_steps - 1)
    def _():
        lam = _diff_lambda(lam_ref, lam_init)
        on = acc / l
        for h in range(n_heads):
            o = on[h * rows:h * rows + n_tok] - lam * on[h * rows + n_tok:(h + 1) * rows]
            ms = jnp.mean(o * o, axis=-1, keepdims=True)
            o_ref[:, h * LANES:(h + 1) * LANES] = o * lax.rsqrt(ms + EPS) * gsub_ref[...] * (1.0 - lam_init)


def _diff_sample_attn(page_table, q, k_new, v_new, lam_vec, gsub_row, k_cache, v_cache, *, layer, lam_init):
    n_seq, n_pages = page_table.shape
    n_tok = q.shape[0] // n_seq
    page_rows = k_cache.shape[2]
    pages_per_step = 8
    n_steps = n_pages // pages_per_step
    assert 2 * n_tok * (D_MODEL // LANES) == LANES and n_pages % pages_per_step == 0
    row_spec = pl.BlockSpec((n_tok, D_MODEL), lambda b, j, pt: (b, 0))

    def page_spec(i):
        return pl.BlockSpec((None, None, page_rows, LANES),
                            lambda b, j, pt: (layer, pt[b, pages_per_step * j + i], 0, 0))

    def const(shape):
        zeros = (0,) * len(shape)
        return pl.BlockSpec(shape, lambda b, j, pt: zeros)

    pages = [page_spec(i) for i in range(pages_per_step)]
    grid_spec = pltpu.PrefetchScalarGridSpec(
        num_scalar_prefetch=1,
        grid=(n_seq, n_steps),
        in_specs=[row_spec, row_spec, row_spec, const((4, HEAD_DIM)), const((1, LANES))] + pages + pages,
        out_specs=row_spec,
        scratch_shapes=[pltpu.VMEM((LANES, LANES), BF16),
                        pltpu.VMEM((LANES, 1), F32),
                        pltpu.VMEM((LANES, 1), F32),
                        pltpu.VMEM((LANES, LANES), F32)])
    return pl.pallas_call(
        functools.partial(_diff_sample_kernel, pages_per_step=pages_per_step, n_steps=n_steps,
                          lam_init=lam_init),
        grid_spec=grid_spec,
        out_shape=jax.ShapeDtypeStruct(q.shape, F32),
        compiler_params=_params(("parallel", "arbitrary")),
        name="diff_sample_attn",
    )(page_table, q, k_new, v_new, lam_vec, gsub_row,
      *([k_cache] * pages_per_step), *([v_cache] * pages_per_step))


def kernel(x_prompt, x_sample, cache_moba_k, cache_moba_v, cache_diff_k, cache_diff_v, page_table, norm_mix, norm_mlp, moba_w_qkv, moba_w_o, moba_q_norm, moba_k_norm, diff_w_qkv, diff_w_o, diff_q_norm, diff_k_norm, diff_lambda, diff_subln, mlp_w_up, mlp_w_down):
    batch, seq, d = x_prompt.shape
    n_seq, n_tok, _ = x_sample.shape
    depth = norm_mix.shape[0]
    n_lay, n_phys, page, a_heads, _ = cache_moba_k.shape
    b_heads = cache_diff_k.shape[3]
    past_len = page_table.shape[1] * page
    tm = 256

    rope_p = _rope_tables(jnp.arange(seq))
    rope_s = _rope_tables(past_len + jnp.arange(n_seq * n_tok) % n_tok)

    moba_kc = jnp.transpose(cache_moba_k, (0, 1, 3, 4, 2)).reshape(n_lay, n_phys, d, page)
    moba_vc = jnp.transpose(cache_moba_v, (0, 1, 3, 4, 2)).reshape(n_lay, n_phys, d, page)
    diff_kc = cache_diff_k.reshape(cache_diff_k.shape[0], n_phys, page * b_heads, 2 * HEAD_DIM)
    diff_vc = cache_diff_v.reshape(cache_diff_v.shape[0], n_phys, page * b_heads, 2 * HEAD_DIM)

    def two_heads(gain):
        return jnp.tile(gain, LANES // HEAD_DIM).reshape(1, LANES)

    xp = x_prompt.reshape(batch * seq, d)
    xs = x_sample.reshape(n_seq * n_tok, d)
    mk_p, mv_p, mk_s, mv_s = [], [], [], []
    dk_p, dv_p, dk_s, dv_s = [], [], [], []
    for i in range(depth):
        j = i // N_MIXERS
        moba = i % N_MIXERS == 0
        w_qkv = (moba_w_qkv if moba else diff_w_qkv)[j].astype(BF16)
        w_o = (moba_w_o if moba else diff_w_o)[j].astype(BF16)
        gq = two_heads((moba_q_norm if moba else diff_q_norm)[j])
        gk = two_heads((moba_k_norm if moba else diff_k_norm)[j])
        g_mix = norm_mix[i].reshape(1, d)
        qp, kp, vp = _qkv_proj(xp, g_mix, w_qkv, gq, gk, rope_p, seq_tiles=seq // tm, transpose_kv=moba, tm=tm)
        qs, ks, vs = _qkv_proj(xs, g_mix, w_qkv, gq, gk, rope_s, seq_tiles=1, transpose_kv=False,
                               tm=n_seq * n_tok)
        if moba:
            op = _moba_prompt_attn(qp, kp, vp)
            os_ = _moba_sample_attn(page_table, qs, ks, vs, moba_kc, moba_vc, layer=j)
            mk_p.append(kp); mv_p.append(vp); mk_s.append(ks); mv_s.append(vs)
        else:
            lam_init = 0.8 - 0.6 * math.exp(-0.3 * i)
            op = _diff_prompt_attn(qp, kp, vp, diff_lambda[j], diff_subln[j].reshape(LANES, 1),
                                   batch=batch, lam_init=lam_init)
            os_ = _diff_sample_attn(page_table, qs, ks, vs, diff_lambda[j], diff_subln[j].reshape(1, LANES),
                                    diff_kc, diff_vc, layer=j, lam_init=lam_init)
            dk_p.append(kp); dv_p.append(vp); dk_s.append(ks); dv_s.append(vs)
        g_mlp = norm_mlp[i].reshape(1, d)
        w_up, w_dn = mlp_w_up[i].astype(BF16), mlp_w_down[i].astype(BF16)
        xp = _out_mlp(xp, op, w_o, g_mlp, w_up, w_dn, tm=tm)
        xs = _out_mlp(xs, os_, w_o, g_mlp, w_up, w_dn, tm=n_seq * n_tok)

    def moba_prompt_out(parts):
        return jnp.transpose(jnp.stack(parts).reshape(len(parts), batch, a_heads, HEAD_DIM, seq), (0, 1, 4, 2, 3))

    def rows_out(parts, n_b, n_s, heads):
        return jnp.stack(parts).reshape(len(parts), n_b, n_s, heads, d // heads)

    return (xp.reshape(batch, seq, d), xs.reshape(n_seq, n_tok, d),
            moba_prompt_out(mk_p), moba_prompt_out(mv_p),
            rows_out(dk_p, batch, seq, b_heads), rows_out(dv_p, batch, seq, b_heads),
            rows_out(mk_s, n_seq, n_tok, a_heads), rows_out(mv_s, n_seq, n_tok, a_heads),
            rows_out(dk_s, n_seq, n_tok, b_heads), rows_out(dv_s, n_seq, n_tok, b_heads))
```

```python
import functools
import math
from typing import Any, NamedTuple

import jax
import jax.numpy as jnp
from jax import lax
from jax.experimental import pallas as pl
from jax.experimental.pallas import tpu as pltpu

F32 = jnp.float32
BF16 = jnp.bfloat16

D_MODEL = 1024
HEAD_DIM = 64
ROT_DIM = HEAD_DIM // 4
ROPE_THETA = 500000.0
MOBA_BLOCK = 256
MOBA_TOPK = 3
Q_CHUNK_DIFF = 128
EPS = 1e-6
N_MIXERS = 2
LANES = 128
BF16_ROWS = 16
SCALE = HEAD_DIM ** -0.5
LOG2E = math.log2(math.e)
PROMPT_STREAMS = 4
NEG = -1e30
VMEM_LIMIT = 56 * 1024 * 1024

_NT = (((1,), (1,)), ((), ()))


def _dot(a, b):
    return jnp.dot(a, b, preferred_element_type=F32)


def _dot_nt(a, b):
    return lax.dot_general(a, b, _NT, preferred_element_type=F32)


def _split_bf16(x):
    hi = x.astype(BF16)
    lo = (x - hi.astype(F32)).astype(BF16)
    return hi, lo


def _params(sem):
    return pltpu.CompilerParams(dimension_semantics=sem, vmem_limit_bytes=VMEM_LIMIT)


def _const_spec(shape):
    zeros = (0,) * len(shape)
    return pl.BlockSpec(shape, lambda *_: zeros, pipeline_mode=pl.Buffered(1))


def _lane_half(shape):
    return lax.broadcasted_iota(jnp.int32, shape, len(shape) - 1) // HEAD_DIM


def _qkv_kernel(x_ref, g_ref, w_ref, gq_ref, gk_ref, cos_ref, sa_ref, sb_ref, *refs, transpose_kv):
    q_ref, k_ref, v_ref = refs[-3:]
    x = x_ref[...]
    ms = jnp.mean(x * x, axis=-1, keepdims=True)
    h = (x * lax.rsqrt(ms + EPS) * g_ref[...]).astype(BF16)
    y = _dot(h, w_ref[...])
    cos, sa, sb = cos_ref[...], sa_ref[...], sb_ref[...]
    first = _lane_half((1, LANES)) == 0

    def norm_rope(blk, gain):
        sq = blk * blk
        s0 = jnp.sum(jnp.where(first, sq, 0.0), axis=-1, keepdims=True)
        s1 = jnp.sum(jnp.where(first, 0.0, sq), axis=-1, keepdims=True)
        r = lax.rsqrt(jnp.where(first, s0, s1) * (1.0 / HEAD_DIM) + EPS)
        n = blk * r * gain
        return n * cos + pltpu.roll(n, LANES - ROT_DIM // 2, 1) * sa + pltpu.roll(n, ROT_DIM // 2, 1) * sb

    n_tiles = D_MODEL // LANES
    for c in range(n_tiles):
        sl = slice(c * LANES, (c + 1) * LANES)
        q_ref[:, sl] = norm_rope(y[:, sl], gq_ref[...])
        kb = norm_rope(y[:, D_MODEL + c * LANES:D_MODEL + (c + 1) * LANES], gk_ref[...])
        vb = y[:, 2 * D_MODEL + c * LANES:2 * D_MODEL + (c + 1) * LANES]
        if transpose_kv:
            k_ref[sl, :] = kb.T
            v_ref[sl, :] = vb.T
        else:
            k_ref[:, sl] = kb
            v_ref[:, sl] = vb


def _qkv_proj(x, g, w, gq, gk, rope, *, seq_tiles, transpose_kv, tm, slab, n_slabs, prev=None):
    t = x.shape[0]
    n_tiles = t // tm
    cos, sa, sb = rope
    row_spec = pl.BlockSpec((tm, D_MODEL), lambda i: (i, 0))
    rope_spec = pl.BlockSpec((tm, LANES), lambda i: (i % seq_tiles, 0))
    if transpose_kv:
        kv_shape = jax.ShapeDtypeStruct((n_slabs, n_tiles // seq_tiles, D_MODEL, seq_tiles * tm), F32)
        kv_spec = pl.BlockSpec((None, None, D_MODEL, tm), lambda i: (slab, i // seq_tiles, 0, i % seq_tiles))
    else:
        kv_shape = jax.ShapeDtypeStruct((n_slabs, t, D_MODEL), F32)
        kv_spec = pl.BlockSpec((None, tm, D_MODEL), lambda i: (slab, i, 0))
    in_specs = [row_spec, _const_spec((1, D_MODEL)), _const_spec((D_MODEL, 3 * D_MODEL)),
                _const_spec((1, LANES)), _const_spec((1, LANES)), rope_spec, rope_spec, rope_spec]
    args = (x, g, w, gq, gk, cos, sa, sb)
    aliases = {}
    if prev is not None:
        aliases = {len(args): 1, len(args) + 1: 2}
        in_specs = in_specs + [pl.BlockSpec(memory_space=pl.ANY)] * 2
        args = args + tuple(prev)
    return pl.pallas_call(
        functools.partial(_qkv_kernel, transpose_kv=transpose_kv),
        grid=(n_tiles,),
        in_specs=in_specs,
        out_specs=[row_spec, kv_spec, kv_spec],
        out_shape=[jax.ShapeDtypeStruct((t, D_MODEL), F32), kv_shape, kv_shape],
        input_output_aliases=aliases,
        compiler_params=_params(("parallel",)),
        name="qkv_proj_t" if transpose_kv else "qkv_proj",
    )(*args)


def _rope_tables(pos):
    half = ROT_DIM // 2
    inv_freq = ROPE_THETA ** (-jnp.arange(half, dtype=F32) * (2.0 / ROT_DIM))
    ang = pos.astype(F32)[:, None] * inv_freq[None, :]
    lane = jnp.arange(LANES)
    d = lane % HEAD_DIM
    cos_l = jnp.cos(ang)[:, lane % half]
    sin_l = jnp.sin(ang)[:, lane % half]
    cos = jnp.where(d < ROT_DIM, cos_l, 1.0)
    sa = jnp.where(d < half, -sin_l, 0.0)
    sb = jnp.where((d >= half) & (d < ROT_DIM), sin_l, 0.0)
    return cos.astype(F32), sa.astype(F32), sb.astype(F32)


def _mlp_kernel(x_ref, o_ref, wo_ref, g_ref, wup_ref, wdn_ref, out_ref, *, ff_chunk):
    x1 = x_ref[...] + _dot(o_ref[...].astype(BF16), wo_ref[...])
    ms = jnp.mean(x1 * x1, axis=-1, keepdims=True)
    h = (x1 * lax.rsqrt(ms + EPS) * g_ref[...]).astype(BF16)
    acc = x1
    for c in range(wup_ref.shape[1] // ff_chunk):
        u = jnp.maximum(_dot(h, wup_ref[:, c * ff_chunk:(c + 1) * ff_chunk]), 0.0)
        acc = acc + _dot((u * u).astype(BF16), wdn_ref[c * ff_chunk:(c + 1) * ff_chunk, :])
    out_ref[...] = acc


def _out_mlp(x, o, wo, g, wup, wdn, *, tm):
    t = x.shape[0]
    d_ff = wup.shape[1]
    row_spec = pl.BlockSpec((tm, D_MODEL), lambda i: (i, 0))
    return pl.pallas_call(
        functools.partial(_mlp_kernel, ff_chunk=1024),
        grid=(t // tm,),
        in_specs=[row_spec, row_spec, _const_spec((D_MODEL, D_MODEL)), _const_spec((1, D_MODEL)),
                  _const_spec((D_MODEL, d_ff)), _const_spec((d_ff, D_MODEL))],
        out_specs=row_spec,
        out_shape=jax.ShapeDtypeStruct((t, D_MODEL), F32),
        compiler_params=_params(("parallel",)),
        name="out_mlp",
    )(x, o, wo, g, wup, wdn)


def _causal_mask(tq):
    key = lax.broadcasted_iota(jnp.int32, (tq, tq), 0)
    qry = lax.broadcasted_iota(jnp.int32, (tq, tq), 1)
    return key <= qry


def _softmax_stream(k_scr, qb, s_scr, p_scr, n_blocks, bias, causal, finish):
    blk = k_scr.shape[1]
    st = {}

    def score(n):
        s = _dot_nt(k_scr[n], qb)
        if n == n_blocks - 1:
            s = jnp.where(causal, s, NEG)
        elif bias is not None:
            s = s + bias[n]
        s_scr[n] = s
        cm = jnp.max(s, axis=0, keepdims=True)
        st["m"] = cm if n == 0 else jnp.maximum(st["m"], cm)

    def prob(n):
        p = jnp.exp2(s_scr[n] - st["m"])
        cs = jnp.sum(p, axis=0, keepdims=True)
        st["l"] = cs if n == 0 else st["l"] + cs
        p_scr[n * blk:(n + 1) * blk, :] = p.astype(BF16)

    stages = [functools.partial(score, n) for n in range(n_blocks)]
    stages += [functools.partial(prob, n) for n in range(n_blocks)]
    return stages + [lambda: finish(st["l"])]


def _run_staggered(streams, lag):
    total = max(i * lag + len(s) for i, s in enumerate(streams))
    for t in range(total):
        for i, s in enumerate(streams):
            if 0 <= t - i * lag < len(s):
                s[t - i * lag]()


def _moba_bias(km_scr, qa, n_past):
    km = km_scr[...]
    n_blk = km.shape[0]
    pad = -n_blk % BF16_ROWS
    kmh, kml = _split_bf16(jnp.concatenate([km, jnp.zeros((pad, km.shape[1]), F32)], axis=0))
    qh, ql = _split_bf16(qa)
    gate = (_dot_nt(kmh, qh) + _dot_nt(kmh, ql) + _dot_nt(kml, qh))[:n_blk]
    blk_id = lax.broadcasted_iota(jnp.int32, (n_blk, 1), 0)
    past = blk_id < n_past
    gate = jnp.where(past, gate, -jnp.inf)
    bias = []
    for n in range(n_past):
        gn = gate[n:n + 1, :]
        beats = past & ((gate > gn) | ((gate == gn) & (blk_id < n)))
        rank = jnp.sum(beats.astype(F32), axis=0, keepdims=True)
        bias.append(jnp.where(rank < MOBA_TOPK, 0.0, NEG))
    return bias


def _moba_prompt_kernel(q_ref, kt_ref, vt_ref, o_ref, k_scr, vt_scr, km_scr, s_scr, p_scr, *, n_blk):
    qi = pl.program_id(2)
    blk = MOBA_BLOCK
    n_pairs = q_ref.shape[1] // LANES

    @pl.when(qi == 0)
    def _():
        for c in range(n_pairs):
            for n in range(n_blk):
                kn = kt_ref[c * LANES:(c + 1) * LANES, n * blk:(n + 1) * blk].T
                km_scr[c, n:n + 1, :] = jnp.mean(kn, axis=0, keepdims=True)
                k_scr[c, n] = kn.astype(BF16)
        vt_scr[...] = vt_ref[...].astype(BF16)

    def tile(k):
        half = _lane_half((1, LANES))
        causal = _causal_mask(blk)
        n_keys = (k + 1) * blk
        outs = [None] * (2 * n_pairs)
        streams = []
        for c in range(n_pairs):
            q = q_ref[:, c * LANES:(c + 1) * LANES]
            for a in range(2):
                i = 2 * c + a
                qa = jnp.where(half == a, q, 0.0)
                bias = _moba_bias(km_scr.at[c], qa, k) if k > MOBA_TOPK else None
                qb = (qa * (SCALE * LOG2E)).astype(BF16)

                def finish(l, i=i):
                    acc = _dot(vt_scr[i * HEAD_DIM:(i + 1) * HEAD_DIM, :n_keys], p_scr[i, :n_keys, :])
                    outs[i] = acc / l

                streams.append(_softmax_stream(k_scr.at[c], qb, s_scr.at[i], p_scr.at[i], k + 1, bias,
                                               causal, finish))
        _run_staggered(streams, lag=k + 1)
        for c in range(n_pairs):
            o_ref[:, c * LANES:(c + 1) * LANES] = jnp.concatenate(
                outs[2 * c:2 * c + 2], axis=0).T.astype(o_ref.dtype)

    for k in range(n_blk):
        pl.when(qi == k)(functools.partial(tile, k))


class _Part(NamedTuple):
    body: Any
    args: tuple
    in_specs: list
    out_spec: Any
    out_shape: Any
    scratch: list


def _prompt_grid(batch, s, tq):
    return (batch, D_MODEL // (PROMPT_STREAMS // 2 * LANES), s // tq)


def _moba_prompt_part(q, kt, vt, *, layer):
    _, b, _, s = kt.shape
    n_blk = s // MOBA_BLOCK
    tq = MOBA_BLOCK
    width = PROMPT_STREAMS // 2 * LANES
    n_streams = PROMPT_STREAMS
    q_spec = pl.BlockSpec((tq, width), lambda bi, hp, qi, pt: (bi * n_blk + qi, hp))
    kv_spec = pl.BlockSpec((None, None, width, s), lambda bi, hp, qi, pt: (layer, bi, hp, 0))
    return _Part(
        body=functools.partial(_moba_prompt_kernel, n_blk=n_blk),
        args=(q, kt, vt),
        in_specs=[q_spec, kv_spec, kv_spec],
        out_spec=q_spec,
        out_shape=jax.ShapeDtypeStruct((b * s, D_MODEL), BF16),
        scratch=[pltpu.VMEM((width // LANES, n_blk, MOBA_BLOCK, LANES), BF16),
                 pltpu.VMEM((width, s), BF16),
                 pltpu.VMEM((width // LANES, n_blk, LANES), F32),
                 pltpu.VMEM((n_streams, n_blk, MOBA_BLOCK, tq), F32),
                 pltpu.VMEM((n_streams, s, tq), BF16)])


def _diff_lambda(lam_ref, lam_init):
    lv = lam_ref[...]
    a = jnp.sum(lv[0:1] * lv[1:2], axis=-1, keepdims=True)
    b = jnp.sum(lv[2:3] * lv[3:4], axis=-1, keepdims=True)
    return jnp.exp(a) - jnp.exp(b) + lam_init


def _diff_prompt_kernel(q_ref, k_ref, v_ref, lam_ref, gsub_ref, o_ref, k_scr, vt_scr, s_scr, p_scr,
                        *, n_blk, lam_init):
    qi = pl.program_id(2)
    tq = q_ref.shape[0]
    n_heads = q_ref.shape[1] // LANES

    @pl.when(qi == 0)
    def _():
        for h in range(n_heads):
            for n in range(n_blk):
                k_scr[h, n] = k_ref[n * tq:(n + 1) * tq, h * LANES:(h + 1) * LANES].astype(BF16)
                vt_scr[h * LANES:(h + 1) * LANES, n * tq:(n + 1) * tq] = (
                    v_ref[n * tq:(n + 1) * tq, h * LANES:(h + 1) * LANES].T.astype(BF16))

    def tile(k):
        half = _lane_half((1, LANES))
        causal = _causal_mask(tq)
        n_keys = (k + 1) * tq
        outs = [None] * (2 * n_heads)
        streams = []
        for h in range(n_heads):
            q = q_ref[:, h * LANES:(h + 1) * LANES]
            for c in range(2):
                i = 2 * h + c
                qb = (jnp.where(half == c, q, 0.0) * (SCALE * LOG2E)).astype(BF16)

                def finish(l, i=i, h=h):
                    outs[i] = _dot(vt_scr[h * LANES:(h + 1) * LANES, :n_keys], p_scr[i, :n_keys, :]) / l

                streams.append(_softmax_stream(k_scr.at[h], qb, s_scr.at[i], p_scr.at[i], k + 1, None,
                                               causal, finish))
        _run_staggered(streams, lag=k + 1)
        lam = _diff_lambda(lam_ref, lam_init)
        for h in range(n_heads):
            ot = outs[2 * h] - lam * outs[2 * h + 1]
            ms = jnp.mean(ot * ot, axis=0, keepdims=True)
            ot = ot * lax.rsqrt(ms + EPS) * gsub_ref[...] * (1.0 - lam_init)
            o = ot.T.astype(o_ref.dtype)
            for c in range(o_ref.shape[0]):
                o_ref[c, :, h * LANES:(h + 1) * LANES] = o[c * Q_CHUNK_DIFF:(c + 1) * Q_CHUNK_DIFF]

    for k in range(n_blk):
        pl.when(qi == k)(functools.partial(tile, k))


def _diff_prompt_part(q, k, v, lam_vec, gsub_col, *, batch, layer, lam_init):
    t = q.shape[0]
    s = t // batch
    tq = MOBA_BLOCK
    assert s % tq == 0 and tq % Q_CHUNK_DIFF == 0
    n_blk = s // tq
    n_streams = PROMPT_STREAMS
    width = n_streams // 2 * LANES
    q_spec = pl.BlockSpec((tq, width), lambda bi, h, qi, pt: (bi * n_blk + qi, h))
    kv_spec = pl.BlockSpec((None, s, width), lambda bi, h, qi, pt: (layer, bi, h))
    per_tile = tq // Q_CHUNK_DIFF
    o_spec = pl.BlockSpec((per_tile, None, Q_CHUNK_DIFF, width), lambda bi, h, qi, pt: (qi, bi, 0, h))
    return _Part(
        body=functools.partial(_diff_prompt_kernel, n_blk=n_blk, lam_init=lam_init),
        args=(q, k, v, lam_vec, gsub_col),
        in_specs=[q_spec, kv_spec, kv_spec, _const_spec((4, HEAD_DIM)), _const_spec((LANES, 1))],
        out_spec=o_spec,
        out_shape=jax.ShapeDtypeStruct((s // Q_CHUNK_DIFF, batch, Q_CHUNK_DIFF, D_MODEL), BF16),
        scratch=[pltpu.VMEM((width // LANES, n_blk, tq, LANES), BF16),
                 pltpu.VMEM((width, s), BF16),
                 pltpu.VMEM((n_streams, n_blk, tq, tq), F32),
                 pltpu.VMEM((n_streams, s, tq), BF16)])


def _pad_rows16(x):
    return jnp.concatenate([x, jnp.zeros_like(x)], axis=0).astype(BF16)


def _moba_sample_kernel(pt_ref, q_ref, kn_ref, vn_ref, *refs, n_blk, blocks_per_step, pages_per_block):
    del pt_ref
    n_pages = blocks_per_step * pages_per_block
    k_refs, v_refs = refs[:n_pages], refs[n_pages:2 * n_pages]
    o_ref, qp_scr, a_scr, g_scr, m_scr, l_scr = refs[2 * n_pages:]
    j = pl.program_id(2)
    n_tok = q_ref.shape[0]
    n_pairs = D_MODEL // LANES
    rows = 2 * n_tok
    half = _lane_half((1, LANES))

    @pl.when(j == 0)
    def _():
        for c in range(n_pairs):
            x = q_ref[:, c * LANES:(c + 1) * LANES] * SCALE
            qp = jnp.concatenate([jnp.where(half == 0, x, 0.0), jnp.where(half == 1, x, 0.0)], axis=0)
            hi, lo = _split_bf16(qp)
            qp_scr[c] = jnp.concatenate([hi, lo], axis=0)
        g_scr[...] = jnp.zeros_like(g_scr)
        m_scr[...] = jnp.zeros_like(m_scr)
        l_scr[...] = jnp.zeros_like(l_scr)

    lane_id = lax.broadcasted_iota(jnp.int32, (1, LANES), 1)
    for t in range(blocks_per_step):
        pages = range(t * pages_per_block, (t + 1) * pages_per_block)
        s_parts = []
        for c in range(n_pairs):
            kp = jnp.concatenate([k_refs[i][c * LANES:(c + 1) * LANES, :] for i in pages],
                                 axis=1).astype(BF16)
            s2 = _dot(qp_scr[c], kp)
            s_parts.append(s2[:rows] + s2[rows:])
        s = jnp.concatenate(s_parts, axis=0)
        gate = jnp.mean(s, axis=-1, keepdims=True)
        m = jnp.max(s, axis=-1, keepdims=True)
        p = jnp.exp(s - m)
        l = jnp.sum(p, axis=-1, keepdims=True)
        pb = p.astype(BF16)
        a_parts = []
        for c in range(n_pairs):
            vp = jnp.concatenate([v_refs[i][c * LANES:(c + 1) * LANES, :] for i in pages],
                                 axis=1).astype(BF16)
            a_parts.append(_dot_nt(pb[c * rows:(c + 1) * rows], vp))
        blk = j * blocks_per_step + t
        a_scr[blk] = jnp.concatenate(a_parts, axis=0)
        here = lane_id == blk
        g_scr[...] = jnp.where(here, gate, g_scr[...])
        m_scr[...] = jnp.where(here, m, m_scr[...])
        l_scr[...] = jnp.where(here, l, l_scr[...])

    @pl.when(j == n_blk // blocks_per_step - 1)
    def _():
        lane = lane_id.astype(F32)
        g = jnp.where(lane < n_blk, g_scr[...], -jnp.inf)
        sel = jnp.zeros(g.shape, jnp.bool_)
        for _ in range(MOBA_TOPK):
            mx = jnp.max(g, axis=-1, keepdims=True)
            idx = jnp.min(jnp.where(g == mx, lane, float(LANES)), axis=-1, keepdims=True)
            pick = lane == idx
            sel = sel | pick
            g = jnp.where(pick, -jnp.inf, g)
        kn, vn = _pad_rows16(kn_ref[...]), _pad_rows16(vn_ref[...])
        qrow = lax.broadcasted_iota(jnp.int32, (rows, rows), 0) % n_tok
        kcol = lax.broadcasted_iota(jnp.int32, (rows, rows), 1)
        so = jnp.concatenate(
            [jnp.where(kcol <= qrow, _dot_nt(qp_scr[c, :rows, :], kn[:, c * LANES:(c + 1) * LANES]), NEG)
             for c in range(n_pairs)], axis=0)
        m_own = jnp.max(so, axis=-1, keepdims=True)
        m_sel = jnp.where(sel, m_scr[...], NEG)
        m_all = jnp.maximum(m_own, jnp.max(m_sel, axis=-1, keepdims=True))
        w = jnp.where(sel, jnp.exp(m_sel - m_all), 0.0)
        po = jnp.exp(so - m_all)
        l_all = jnp.sum(w * l_scr[...], axis=-1, keepdims=True) + jnp.sum(po, axis=-1, keepdims=True)
        pob = po.astype(BF16)
        acc = jnp.concatenate(
            [_dot(pob[c * rows:(c + 1) * rows], vn[:, c * LANES:(c + 1) * LANES]) for c in range(n_pairs)],
            axis=0)
        for n in range(n_blk):
            acc = acc + a_scr[n] * w[:, n:n + 1]
        acc = acc / l_all
        for c in range(n_pairs):
            o_ref[:, c * LANES:(c + 1) * LANES] = jnp.where(
                half == 0, acc[c * rows:c * rows + n_tok], acc[c * rows + n_tok:(c + 1) * rows])


def _moba_sample_part(page_table, q, k_new, v_new, k_cache, v_cache, *, layer, grid):
    n_seq, n_pages = page_table.shape
    n_tok = q.shape[0] // n_seq
    page = k_cache.shape[-1]
    per_blk = MOBA_BLOCK // page
    n_blk = n_pages // per_blk
    _, groups, n_steps = grid
    assert n_seq == grid[0] * groups and n_blk % n_steps == 0
    blocks_per_step = n_blk // n_steps
    assert n_tok * (D_MODEL // HEAD_DIM) == LANES and n_blk <= LANES
    per_step = per_blk * blocks_per_step
    row_spec = pl.BlockSpec((n_tok, D_MODEL), lambda bi, g, j, pt: (bi * groups + g, 0))

    def page_spec(i):
        return pl.BlockSpec((None, None, D_MODEL, page),
                            lambda bi, g, j, pt: (layer, pt[bi * groups + g, per_step * j + i], 0, 0))

    pages = [page_spec(i) for i in range(per_step)]
    return _Part(
        body=functools.partial(_moba_sample_kernel, n_blk=n_blk, blocks_per_step=blocks_per_step,
                               pages_per_block=per_blk),
        args=(q, k_new, v_new, *([k_cache] * per_step), *([v_cache] * per_step)),
        in_specs=[row_spec, row_spec, row_spec] + pages + pages,
        out_spec=row_spec,
        out_shape=jax.ShapeDtypeStruct(q.shape, F32),
        scratch=[pltpu.VMEM((D_MODEL // LANES, 4 * n_tok, LANES), BF16),
                 pltpu.VMEM((n_blk, LANES, LANES), F32),
                 pltpu.VMEM((LANES, LANES), F32),
                 pltpu.VMEM((LANES, LANES), F32),
                 pltpu.VMEM((LANES, LANES), F32)])


def _diff_sample_kernel(pt_ref, q_ref, kn_ref, vn_ref, lam_ref, gsub_ref, *refs,
                        pages_per_step, n_steps, lam_init):
    del pt_ref
    k_refs = refs[:pages_per_step]
    v_refs = refs[pages_per_step:2 * pages_per_step]
    o_ref, qd_scr, m_scr, l_scr, acc_scr = refs[2 * pages_per_step:]
    j = pl.program_id(2)
    n_tok = q_ref.shape[0]
    n_heads = D_MODEL // LANES
    page = k_refs[0].shape[0] // n_heads
    rows = 2 * n_tok
    half = _lane_half((1, LANES))

    def head_rows(page_refs, h):
        return jnp.concatenate([r[pl.ds(h, page, stride=n_heads), :] for r in page_refs], axis=0).astype(BF16)

    @pl.when(j == 0)
    def _():
        for h in range(n_heads):
            x = q_ref[:, h * LANES:(h + 1) * LANES] * SCALE
            qd_scr[h * rows:(h + 1) * rows, :] = jnp.concatenate(
                [jnp.where(half == 0, x, 0.0), jnp.where(half == 1, x, 0.0)], axis=0).astype(BF16)
        kn, vn = _pad_rows16(kn_ref[...]), _pad_rows16(vn_ref[...])
        qrow = lax.broadcasted_iota(jnp.int32, (rows, rows), 0) % n_tok
        kcol = lax.broadcasted_iota(jnp.int32, (rows, rows), 1)
        so = jnp.concatenate(
            [jnp.where(kcol <= qrow,
                       _dot_nt(qd_scr[h * rows:(h + 1) * rows, :], kn[:, h * LANES:(h + 1) * LANES]), NEG)
             for h in range(n_heads)], axis=0)
        m = jnp.max(so, axis=-1, keepdims=True)
        po = jnp.exp(so - m)
        pob = po.astype(BF16)
        m_scr[...] = m
        l_scr[...] = jnp.sum(po, axis=-1, keepdims=True)
        acc_scr[...] = jnp.concatenate(
            [_dot(pob[h * rows:(h + 1) * rows], vn[:, h * LANES:(h + 1) * LANES]) for h in range(n_heads)],
            axis=0)

    s = jnp.concatenate([_dot_nt(qd_scr[h * rows:(h + 1) * rows, :], head_rows(k_refs, h))
                         for h in range(n_heads)], axis=0)
    m = m_scr[...]
    m_new = jnp.maximum(m, jnp.max(s, axis=-1, keepdims=True))
    alpha = jnp.exp(m - m_new)
    p = jnp.exp(s - m_new)
    pb = p.astype(BF16)
    pv = jnp.concatenate([_dot(pb[h * rows:(h + 1) * rows], head_rows(v_refs, h)) for h in range(n_heads)],
                         axis=0)
    m_scr[...] = m_new
    l = alpha * l_scr[...] + jnp.sum(p, axis=-1, keepdims=True)
    acc = alpha * acc_scr[...] + pv
    l_scr[...] = l
    acc_scr[...] = acc

    @pl.when(j == n_steps - 1)
    def _():
        lam = _diff_lambda(lam_ref, lam_init)
        on = acc / l
        for h in range(n_heads):
            o = on[h * rows:h * rows + n_tok] - lam * on[h * rows + n_tok:(h + 1) * rows]
            ms = jnp.mean(o * o, axis=-1, keepdims=True)
            o_ref[:, h * LANES:(h + 1) * LANES] = o * lax.rsqrt(ms + EPS) * gsub_ref[...] * (1.0 - lam_init)


def _diff_sample_part(page_table, q, k_new, v_new, lam_vec, gsub_row, k_cache, v_cache, *, layer, lam_init, grid):
    n_seq, n_pages = page_table.shape
    n_tok = q.shape[0] // n_seq
    page_rows = k_cache.shape[2]
    _, groups, n_steps = grid
    assert n_seq == grid[0] * groups and n_pages % n_steps == 0
    pages_per_step = n_pages // n_steps
    assert 2 * n_tok * (D_MODEL // LANES) == LANES
    row_spec = pl.BlockSpec((n_tok, D_MODEL), lambda bi, g, j, pt: (bi * groups + g, 0))

    def page_spec(i):
        return pl.BlockSpec((None, None, page_rows, LANES),
                            lambda bi, g, j, pt: (layer, pt[bi * groups + g, pages_per_step * j + i], 0, 0))

    def const(shape):
        zeros = (0,) * len(shape)
        return pl.BlockSpec(shape, lambda *_: zeros)

    pages = [page_spec(i) for i in range(pages_per_step)]
    return _Part(
        body=functools.partial(_diff_sample_kernel, pages_per_step=pages_per_step, n_steps=n_steps,
                               lam_init=lam_init),
        args=(q, k_new, v_new, lam_vec, gsub_row, *([k_cache] * pages_per_step), *([v_cache] * pages_per_step)),
        in_specs=[row_spec, row_spec, row_spec, const((4, HEAD_DIM)), const((1, LANES))] + pages + pages,
        out_spec=row_spec,
        out_shape=jax.ShapeDtypeStruct(q.shape, F32),
        scratch=[pltpu.VMEM((LANES, LANES), BF16),
                 pltpu.VMEM((LANES, 1), F32),
                 pltpu.VMEM((LANES, 1), F32),
                 pltpu.VMEM((LANES, LANES), F32)])


def _fused_attn(page_table, prompt, sample, grid, name):
    n_pi, n_si = len(prompt.args), len(sample.args)
    n_ps = len(prompt.scratch)

    def body(pt_ref, *refs):
        p_in, s_in = refs[:n_pi], refs[n_pi:n_pi + n_si]
        p_out, s_out = refs[n_pi + n_si], refs[n_pi + n_si + 1]
        scr = refs[n_pi + n_si + 2:]
        prompt.body(*p_in, p_out, *scr[:n_ps])
        sample.body(pt_ref, *s_in, s_out, *scr[n_ps:])

    grid_spec = pltpu.PrefetchScalarGridSpec(
        num_scalar_prefetch=1,
        grid=grid,
        in_specs=prompt.in_specs + sample.in_specs,
        out_specs=[prompt.out_spec, sample.out_spec],
        scratch_shapes=prompt.scratch + sample.scratch)
    return pl.pallas_call(
        body,
        grid_spec=grid_spec,
        out_shape=[prompt.out_shape, sample.out_shape],
        compiler_params=_params(("parallel", "parallel", "arbitrary")),
        name=name,
    )(page_table, *prompt.args, *sample.args)


def kernel(x_prompt, x_sample, cache_moba_k, cache_moba_v, cache_diff_k, cache_diff_v, page_table, norm_mix, norm_mlp, moba_w_qkv, moba_w_o, moba_q_norm, moba_k_norm, diff_w_qkv, diff_w_o, diff_q_norm, diff_k_norm, diff_lambda, diff_subln, mlp_w_up, mlp_w_down):
    batch, seq, d = x_prompt.shape
    n_seq, n_tok, _ = x_sample.shape
    depth = norm_mix.shape[0]
    n_lay, n_phys, page, a_heads, _ = cache_moba_k.shape
    b_heads = cache_diff_k.shape[3]
    past_len = page_table.shape[1] * page
    tm = 256

    rope_p = _rope_tables(jnp.arange(seq))
    rope_s = _rope_tables(past_len + jnp.arange(n_seq * n_tok) % n_tok)

    moba_kc = jnp.transpose(cache_moba_k, (0, 1, 3, 4, 2)).reshape(n_lay, n_phys, d, page)
    moba_vc = jnp.transpose(cache_moba_v, (0, 1, 3, 4, 2)).reshape(n_lay, n_phys, d, page)
    diff_kc = cache_diff_k.reshape(cache_diff_k.shape[0], n_phys, page * b_heads, 2 * HEAD_DIM)
    diff_vc = cache_diff_v.reshape(cache_diff_v.shape[0], n_phys, page * b_heads, 2 * HEAD_DIM)

    def two_heads(gain):
        return jnp.tile(gain, LANES // HEAD_DIM).reshape(1, LANES)

    xp = x_prompt.reshape(batch * seq, d)
    xs = x_sample.reshape(n_seq * n_tok, d)
    grid = _prompt_grid(batch, seq, MOBA_BLOCK)
    kv_p = {True: None, False: None}
    kv_s = {True: None, False: None}
    for i in range(depth):
        j = i // N_MIXERS
        moba = i % N_MIXERS == 0
        n_slabs = (cache_moba_k if moba else cache_diff_k).shape[0]
        w_qkv = (moba_w_qkv if moba else diff_w_qkv)[j].astype(BF16)
        w_o = (moba_w_o if moba else diff_w_o)[j].astype(BF16)
        gq = two_heads((moba_q_norm if moba else diff_q_norm)[j])
        gk = two_heads((moba_k_norm if moba else diff_k_norm)[j])
        g_mix = norm_mix[i].reshape(1, d)
        qp, *kv_p[moba] = _qkv_proj(xp, g_mix, w_qkv, gq, gk, rope_p, seq_tiles=seq // tm, transpose_kv=moba,
                                    tm=tm, slab=j, n_slabs=n_slabs, prev=kv_p[moba])
        qs, *kv_s[moba] = _qkv_proj(xs, g_mix, w_qkv, gq, gk, rope_s, seq_tiles=1, transpose_kv=False,
                                    tm=n_seq * n_tok, slab=j, n_slabs=n_slabs, prev=kv_s[moba])
        ks, vs = kv_s[moba][0][j], kv_s[moba][1][j]
        if moba:
            prompt = _moba_prompt_part(qp, *kv_p[moba], layer=j)
            sample = _moba_sample_part(page_table, qs, ks, vs, moba_kc, moba_vc, layer=j, grid=grid)
            op, os_ = _fused_attn(page_table, prompt, sample, grid, "moba_attn")
        else:
            lam_init = 0.8 - 0.6 * math.exp(-0.3 * i)
            prompt = _diff_prompt_part(qp, *kv_p[moba], diff_lambda[j], diff_subln[j].reshape(LANES, 1),
                                       batch=batch, layer=j, lam_init=lam_init)
            sample = _diff_sample_part(page_table, qs, ks, vs, diff_lambda[j], diff_subln[j].reshape(1, LANES),
                                       diff_kc, diff_vc, layer=j, lam_init=lam_init, grid=grid)
            op, os_ = _fused_attn(page_table, prompt, sample, grid, "diff_attn")
            op = op.reshape(batch * seq, d)
        g_mlp = norm_mlp[i].reshape(1, d)
        w_up, w_dn = mlp_w_up[i].astype(BF16), mlp_w_down[i].astype(BF16)
        xp = _out_mlp(xp, op, w_o, g_mlp, w_up, w_dn, tm=tm)
        xs = _out_mlp(xs, os_, w_o, g_mlp, w_up, w_dn, tm=n_seq * n_tok)

    def moba_prompt_out(stack):
        return jnp.transpose(stack.reshape(stack.shape[0], batch, a_heads, HEAD_DIM, seq), (0, 1, 4, 2, 3))

    def rows_out(stack, n_b, n_s, heads):
        return stack.reshape(stack.shape[0], n_b, n_s, heads, d // heads)

    return (xp.reshape(batch, seq, d), xs.reshape(n_seq, n_tok, d),
            moba_prompt_out(kv_p[True][0]), moba_prompt_out(kv_p[True][1]),
            rows_out(kv_p[False][0], batch, seq, b_heads), rows_out(kv_p[False][1], batch, seq, b_heads),
            rows_out(kv_s[True][0], n_seq, n_tok, a_heads), rows_out(kv_s[True][1], n_seq, n_tok, a_heads),
            rows_out(kv_s[False][0], n_seq, n_tok, b_heads), rows_out(kv_s[False][1], n_seq, n_tok, b_heads))
```

```python
import functools
import math
from typing import Any, NamedTuple

import jax
import jax.numpy as jnp
from jax import lax
from jax.experimental import pallas as pl
from jax.experimental.pallas import tpu as pltpu

F32 = jnp.float32
BF16 = jnp.bfloat16

D_MODEL = 1024
HEAD_DIM = 64
ROT_DIM = HEAD_DIM // 4
ROPE_THETA = 500000.0
MOBA_BLOCK = 256
MOBA_TOPK = 3
Q_CHUNK_DIFF = 128
EPS = 1e-6
N_MIXERS = 2
LANES = 128
BF16_ROWS = 16
SCALE = HEAD_DIM ** -0.5
LOG2E = math.log2(math.e)
PROMPT_STREAMS = 4
STREAM_BUFS = 2
NEG = -1e30
VMEM_LIMIT = 56 * 1024 * 1024

_NT = (((1,), (1,)), ((), ()))


def _dot(a, b):
    return jnp.dot(a, b, preferred_element_type=F32)


def _dot_nt(a, b):
    return lax.dot_general(a, b, _NT, preferred_element_type=F32)


def _split_bf16(x):
    hi = x.astype(BF16)
    lo = (x - hi.astype(F32)).astype(BF16)
    return hi, lo


def _params(sem):
    return pltpu.CompilerParams(dimension_semantics=sem, vmem_limit_bytes=VMEM_LIMIT)


def _const_spec(shape):
    zeros = (0,) * len(shape)
    return pl.BlockSpec(shape, lambda *_: zeros, pipeline_mode=pl.Buffered(1))


def _lane_half(shape):
    return lax.broadcasted_iota(jnp.int32, shape, len(shape) - 1) // HEAD_DIM


def _qkv_kernel(x_ref, g_ref, w_ref, gq_ref, gk_ref, cos_ref, sa_ref, sb_ref, *refs, transpose_kv):
    q_ref, k_ref, v_ref = refs[-3:]
    x = x_ref[...]
    ms = jnp.mean(x * x, axis=-1, keepdims=True)
    h = (x * lax.rsqrt(ms + EPS) * g_ref[...]).astype(BF16)
    y = _dot(h, w_ref[...])
    cos, sa, sb = cos_ref[...], sa_ref[...], sb_ref[...]
    first = _lane_half((1, LANES)) == 0

    def norm_rope(blk, gain):
        sq = blk * blk
        s0 = jnp.sum(jnp.where(first, sq, 0.0), axis=-1, keepdims=True)
        s1 = jnp.sum(jnp.where(first, 0.0, sq), axis=-1, keepdims=True)
        r = lax.rsqrt(jnp.where(first, s0, s1) * (1.0 / HEAD_DIM) + EPS)
        n = blk * r * gain
        return n * cos + pltpu.roll(n, LANES - ROT_DIM // 2, 1) * sa + pltpu.roll(n, ROT_DIM // 2, 1) * sb

    n_tiles = D_MODEL // LANES
    for c in range(n_tiles):
        sl = slice(c * LANES, (c + 1) * LANES)
        q_ref[:, sl] = norm_rope(y[:, sl], gq_ref[...])
        kb = norm_rope(y[:, D_MODEL + c * LANES:D_MODEL + (c + 1) * LANES], gk_ref[...])
        vb = y[:, 2 * D_MODEL + c * LANES:2 * D_MODEL + (c + 1) * LANES]
        if transpose_kv:
            k_ref[sl, :] = kb.T
            v_ref[sl, :] = vb.T
        else:
            k_ref[:, sl] = kb
            v_ref[:, sl] = vb


def _qkv_proj(x, g, w, gq, gk, rope, *, seq_tiles, transpose_kv, tm, slab, n_slabs, prev=None):
    t = x.shape[0]
    n_tiles = t // tm
    cos, sa, sb = rope
    row_spec = pl.BlockSpec((tm, D_MODEL), lambda i: (i, 0))
    rope_spec = pl.BlockSpec((tm, LANES), lambda i: (i % seq_tiles, 0))
    if transpose_kv:
        kv_shape = jax.ShapeDtypeStruct((n_slabs, n_tiles // seq_tiles, D_MODEL, seq_tiles * tm), F32)
        kv_spec = pl.BlockSpec((None, None, D_MODEL, tm), lambda i: (slab, i // seq_tiles, 0, i % seq_tiles))
    else:
        kv_shape = jax.ShapeDtypeStruct((n_slabs, t, D_MODEL), F32)
        kv_spec = pl.BlockSpec((None, tm, D_MODEL), lambda i: (slab, i, 0))
    in_specs = [row_spec, _const_spec((1, D_MODEL)), _const_spec((D_MODEL, 3 * D_MODEL)),
                _const_spec((1, LANES)), _const_spec((1, LANES)), rope_spec, rope_spec, rope_spec]
    args = (x, g, w, gq, gk, cos, sa, sb)
    aliases = {}
    if prev is not None:
        aliases = {len(args): 1, len(args) + 1: 2}
        in_specs = in_specs + [pl.BlockSpec(memory_space=pl.ANY)] * 2
        args = args + tuple(prev)
    return pl.pallas_call(
        functools.partial(_qkv_kernel, transpose_kv=transpose_kv),
        grid=(n_tiles,),
        in_specs=in_specs,
        out_specs=[row_spec, kv_spec, kv_spec],
        out_shape=[jax.ShapeDtypeStruct((t, D_MODEL), F32), kv_shape, kv_shape],
        input_output_aliases=aliases,
        compiler_params=_params(("parallel",)),
        name="qkv_proj_t" if transpose_kv else "qkv_proj",
    )(*args)


def _rope_tables(pos):
    half = ROT_DIM // 2
    inv_freq = ROPE_THETA ** (-jnp.arange(half, dtype=F32) * (2.0 / ROT_DIM))
    ang = pos.astype(F32)[:, None] * inv_freq[None, :]
    lane = jnp.arange(LANES)
    d = lane % HEAD_DIM
    cos_l = jnp.cos(ang)[:, lane % half]
    sin_l = jnp.sin(ang)[:, lane % half]
    cos = jnp.where(d < ROT_DIM, cos_l, 1.0)
    sa = jnp.where(d < half, -sin_l, 0.0)
    sb = jnp.where((d >= half) & (d < ROT_DIM), sin_l, 0.0)
    return cos.astype(F32), sa.astype(F32), sb.astype(F32)


def _mlp_kernel(x_ref, o_ref, wo_ref, g_ref, wup_ref, wdn_ref, out_ref, *, ff_chunk):
    x1 = x_ref[...] + _dot(o_ref[...].astype(BF16), wo_ref[...])
    ms = jnp.mean(x1 * x1, axis=-1, keepdims=True)
    h = (x1 * lax.rsqrt(ms + EPS) * g_ref[...]).astype(BF16)
    acc = x1
    for c in range(wup_ref.shape[1] // ff_chunk):
        u = jnp.maximum(_dot(h, wup_ref[:, c * ff_chunk:(c + 1) * ff_chunk]), 0.0)
        acc = acc + _dot((u * u).astype(BF16), wdn_ref[c * ff_chunk:(c + 1) * ff_chunk, :])
    out_ref[...] = acc


def _out_mlp(x, o, wo, g, wup, wdn, *, tm):
    t = x.shape[0]
    d_ff = wup.shape[1]
    row_spec = pl.BlockSpec((tm, D_MODEL), lambda i: (i, 0))
    return pl.pallas_call(
        functools.partial(_mlp_kernel, ff_chunk=1024),
        grid=(t // tm,),
        in_specs=[row_spec, row_spec, _const_spec((D_MODEL, D_MODEL)), _const_spec((1, D_MODEL)),
                  _const_spec((D_MODEL, d_ff)), _const_spec((d_ff, D_MODEL))],
        out_specs=row_spec,
        out_shape=jax.ShapeDtypeStruct((t, D_MODEL), F32),
        compiler_params=_params(("parallel",)),
        name="out_mlp",
    )(x, o, wo, g, wup, wdn)


def _causal_mask(tq):
    key = lax.broadcasted_iota(jnp.int32, (tq, tq), 0)
    qry = lax.broadcasted_iota(jnp.int32, (tq, tq), 1)
    return key <= qry


def _softmax_stream(k_scr, qb, s_scr, p_scr, n_blocks, bias, causal, finish):
    blk = k_scr.shape[1]
    st = {}

    def score(n):
        s = _dot_nt(k_scr[n], qb)
        if n == n_blocks - 1:
            s = jnp.where(causal, s, NEG)
        elif bias is not None:
            s = s + bias[n]
        s_scr[n] = s
        cm = jnp.max(s, axis=0, keepdims=True)
        st["m"] = cm if n == 0 else jnp.maximum(st["m"], cm)

    def prob(n):
        p = jnp.exp2(s_scr[n] - st["m"])
        cs = jnp.sum(p, axis=0, keepdims=True)
        st["l"] = cs if n == 0 else st["l"] + cs
        p_scr[n * blk:(n + 1) * blk, :] = p.astype(BF16)

    stages = [functools.partial(score, n) for n in range(n_blocks)]
    stages += [functools.partial(prob, n) for n in range(n_blocks)]
    return stages + [lambda: finish(st["l"])]


def _run_staggered(streams, lag):
    total = max(i * lag + len(s) for i, s in enumerate(streams))
    for t in range(total):
        for i, s in enumerate(streams):
            if 0 <= t - i * lag < len(s):
                s[t - i * lag]()


def _moba_bias(km_scr, qa, n_past):
    km = km_scr[...]
    n_blk = km.shape[0]
    pad = -n_blk % BF16_ROWS
    kmh, kml = _split_bf16(jnp.concatenate([km, jnp.zeros((pad, km.shape[1]), F32)], axis=0))
    qh, ql = _split_bf16(qa)
    gate = (_dot_nt(kmh, qh) + _dot_nt(kmh, ql) + _dot_nt(kml, qh))[:n_blk]
    blk_id = lax.broadcasted_iota(jnp.int32, (n_blk, 1), 0)
    past = blk_id < n_past
    gate = jnp.where(past, gate, -jnp.inf)
    bias = []
    for n in range(n_past):
        gn = gate[n:n + 1, :]
        beats = past & ((gate > gn) | ((gate == gn) & (blk_id < n)))
        rank = jnp.sum(beats.astype(F32), axis=0, keepdims=True)
        bias.append(jnp.where(rank < MOBA_TOPK, 0.0, NEG))
    return bias


def _moba_prompt_kernel(q_ref, kt_ref, vt_ref, o_ref, k_scr, vt_scr, km_scr, s_scr, p_scr, *, n_blk):
    qi = pl.program_id(2)
    blk = MOBA_BLOCK
    n_pairs = q_ref.shape[1] // LANES

    @pl.when(qi == 0)
    def _():
        for c in range(n_pairs):
            for n in range(n_blk):
                kn = kt_ref[c * LANES:(c + 1) * LANES, n * blk:(n + 1) * blk].T
                km_scr[c, n:n + 1, :] = jnp.mean(kn, axis=0, keepdims=True)
                k_scr[c, n] = kn.astype(BF16)
        vt_scr[...] = vt_ref[...].astype(BF16)

    def tile(k):
        half = _lane_half((1, LANES))
        causal = _causal_mask(blk)
        n_keys = (k + 1) * blk
        outs = [None] * (2 * n_pairs)
        streams = []
        for c in range(n_pairs):
            q = q_ref[:, c * LANES:(c + 1) * LANES]
            for a in range(2):
                i = 2 * c + a
                qa = jnp.where(half == a, q, 0.0)
                bias = _moba_bias(km_scr.at[c], qa, k) if k > MOBA_TOPK else None
                qb = (qa * (SCALE * LOG2E)).astype(BF16)

                def finish(l, i=i):
                    acc = _dot(vt_scr[i * HEAD_DIM:(i + 1) * HEAD_DIM, :n_keys], p_scr[i % STREAM_BUFS, :n_keys, :])
                    outs[i] = acc / l

                streams.append(_softmax_stream(k_scr.at[c], qb, s_scr.at[i % STREAM_BUFS], p_scr.at[i % STREAM_BUFS], k + 1, bias,
                                               causal, finish))
        _run_staggered(streams, lag=k + 1)
        for c in range(n_pairs):
            o_ref[:, c * LANES:(c + 1) * LANES] = jnp.concatenate(
                outs[2 * c:2 * c + 2], axis=0).T.astype(o_ref.dtype)

    for k in range(n_blk):
        pl.when(qi == k)(functools.partial(tile, k))


class _Part(NamedTuple):
    body: Any
    args: tuple
    in_specs: list
    out_spec: Any
    out_shape: Any
    scratch: list


def _prompt_grid(batch, s, tq):
    return (batch, D_MODEL // (PROMPT_STREAMS // 2 * LANES), s // tq)


def _moba_prompt_part(q, kt, vt, *, layer):
    _, b, _, s = kt.shape
    n_blk = s // MOBA_BLOCK
    tq = MOBA_BLOCK
    width = PROMPT_STREAMS // 2 * LANES
    n_streams = PROMPT_STREAMS
    q_spec = pl.BlockSpec((tq, width), lambda bi, hp, st, pt: (bi * n_blk + st, hp))
    kv_spec = pl.BlockSpec((None, None, width, s), lambda bi, hp, st, pt: (layer, bi, hp, 0))
    return _Part(
        body=functools.partial(_moba_prompt_kernel, n_blk=n_blk),
        args=(q, kt, vt),
        in_specs=[q_spec, kv_spec, kv_spec],
        out_spec=q_spec,
        out_shape=jax.ShapeDtypeStruct((b * s, D_MODEL), BF16),
        scratch=[pltpu.VMEM((width // LANES, n_blk, MOBA_BLOCK, LANES), BF16),
                 pltpu.VMEM((width, s), BF16),
                 pltpu.VMEM((width // LANES, n_blk, LANES), F32),
                 pltpu.VMEM((STREAM_BUFS, n_blk, MOBA_BLOCK, tq), F32),
                 pltpu.VMEM((STREAM_BUFS, s, tq), BF16)])


def _diff_lambda(lam_ref, lam_init):
    lv = lam_ref[...]
    a = jnp.sum(lv[0:1] * lv[1:2], axis=-1, keepdims=True)
    b = jnp.sum(lv[2:3] * lv[3:4], axis=-1, keepdims=True)
    return jnp.exp(a) - jnp.exp(b) + lam_init


def _diff_prompt_kernel(q_ref, k_ref, v_ref, lam_ref, gsub_ref, o_ref, k_scr, vt_scr, s_scr, p_scr,
                        *, n_blk, lam_init):
    qi = pl.program_id(2)
    tq = q_ref.shape[0]
    n_heads = q_ref.shape[1] // LANES

    @pl.when(qi == 0)
    def _():
        for h in range(n_heads):
            for n in range(n_blk):
                k_scr[h, n] = k_ref[n * tq:(n + 1) * tq, h * LANES:(h + 1) * LANES].astype(BF16)
                vt_scr[h * LANES:(h + 1) * LANES, n * tq:(n + 1) * tq] = (
                    v_ref[n * tq:(n + 1) * tq, h * LANES:(h + 1) * LANES].T.astype(BF16))

    def tile(k):
        half = _lane_half((1, LANES))
        causal = _causal_mask(tq)
        n_keys = (k + 1) * tq
        outs = [None] * (2 * n_heads)
        streams = []
        for h in range(n_heads):
            q = q_ref[:, h * LANES:(h + 1) * LANES]
            for c in range(2):
                i = 2 * h + c
                qb = (jnp.where(half == c, q, 0.0) * (SCALE * LOG2E)).astype(BF16)

                def finish(l, i=i, h=h):
                    outs[i] = _dot(vt_scr[h * LANES:(h + 1) * LANES, :n_keys], p_scr[i % STREAM_BUFS, :n_keys, :]) / l

                streams.append(_softmax_stream(k_scr.at[h], qb, s_scr.at[i % STREAM_BUFS], p_scr.at[i % STREAM_BUFS], k + 1, None,
                                               causal, finish))
        _run_staggered(streams, lag=k + 1)
        lam = _diff_lambda(lam_ref, lam_init)
        for h in range(n_heads):
            ot = outs[2 * h] - lam * outs[2 * h + 1]
            ms = jnp.mean(ot * ot, axis=0, keepdims=True)
            ot = ot * lax.rsqrt(ms + EPS) * gsub_ref[...] * (1.0 - lam_init)
            o = ot.T.astype(o_ref.dtype)
            for c in range(o_ref.shape[0]):
                o_ref[c, :, h * LANES:(h + 1) * LANES] = o[c * Q_CHUNK_DIFF:(c + 1) * Q_CHUNK_DIFF]

    for k in range(n_blk):
        pl.when(qi == k)(functools.partial(tile, k))


def _diff_prompt_part(q, k, v, lam_vec, gsub_col, *, batch, layer, lam_init):
    t = q.shape[0]
    s = t // batch
    tq = MOBA_BLOCK
    assert s % tq == 0 and tq % Q_CHUNK_DIFF == 0
    n_blk = s // tq
    n_streams = PROMPT_STREAMS
    width = n_streams // 2 * LANES
    q_spec = pl.BlockSpec((tq, width), lambda bi, h, st, pt: (bi * n_blk + st, h))
    kv_spec = pl.BlockSpec((None, s, width), lambda bi, h, st, pt: (layer, bi, h))
    per_tile = tq // Q_CHUNK_DIFF
    o_spec = pl.BlockSpec((per_tile, None, Q_CHUNK_DIFF, width),
                          lambda bi, h, st, pt: (st, bi, 0, h))
    return _Part(
        body=functools.partial(_diff_prompt_kernel, n_blk=n_blk, lam_init=lam_init),
        args=(q, k, v, lam_vec, gsub_col),
        in_specs=[q_spec, kv_spec, kv_spec, _const_spec((4, HEAD_DIM)), _const_spec((LANES, 1))],
        out_spec=o_spec,
        out_shape=jax.ShapeDtypeStruct((s // Q_CHUNK_DIFF, batch, Q_CHUNK_DIFF, D_MODEL), BF16),
        scratch=[pltpu.VMEM((width // LANES, n_blk, tq, LANES), BF16),
                 pltpu.VMEM((width, s), BF16),
                 pltpu.VMEM((STREAM_BUFS, n_blk, tq, tq), F32),
                 pltpu.VMEM((STREAM_BUFS, s, tq), BF16)])


def _pad_rows16(x):
    return jnp.concatenate([x, jnp.zeros_like(x)], axis=0).astype(BF16)


def _moba_sample_kernel(pt_ref, q_ref, kn_ref, vn_ref, *refs, n_blk, blocks_per_step, pages_per_block):
    del pt_ref
    n_pages = blocks_per_step * pages_per_block
    k_refs, v_refs = refs[:n_pages], refs[n_pages:2 * n_pages]
    o_ref, qp_scr, a_scr, g_scr, m_scr, l_scr = refs[2 * n_pages:]
    j = pl.program_id(2)
    n_tok = q_ref.shape[0]
    n_pairs = D_MODEL // LANES
    rows = 2 * n_tok
    half = _lane_half((1, LANES))

    @pl.when(j == 0)
    def _():
        for c in range(n_pairs):
            x = q_ref[:, c * LANES:(c + 1) * LANES] * SCALE
            qp = jnp.concatenate([jnp.where(half == 0, x, 0.0), jnp.where(half == 1, x, 0.0)], axis=0)
            hi, lo = _split_bf16(qp)
            qp_scr[c] = jnp.concatenate([hi, lo], axis=0)
        g_scr[...] = jnp.zeros_like(g_scr)
        m_scr[...] = jnp.zeros_like(m_scr)
        l_scr[...] = jnp.zeros_like(l_scr)

    lane_id = lax.broadcasted_iota(jnp.int32, (1, LANES), 1)
    def block_pages(refs, t, c):
        pages = range(t * pages_per_block, (t + 1) * pages_per_block)
        return jnp.concatenate([refs[i][c * LANES:(c + 1) * LANES, :] for i in pages], axis=1).astype(BF16)

    scores = []
    for t in range(blocks_per_step):
        parts = []
        for c in range(n_pairs):
            s2 = _dot(qp_scr[c], block_pages(k_refs, t, c))
            parts.append(s2[:rows] + s2[rows:])
        scores.append(jnp.concatenate(parts, axis=0))
    g, m_all, l_all = g_scr[...], m_scr[...], l_scr[...]
    probs = []
    for t, s in enumerate(scores):
        gate = jnp.mean(s, axis=-1, keepdims=True)
        m = jnp.max(s, axis=-1, keepdims=True)
        p = jnp.exp(s - m)
        l = jnp.sum(p, axis=-1, keepdims=True)
        probs.append(p.astype(BF16))
        here = lane_id == j * blocks_per_step + t
        g, m_all, l_all = jnp.where(here, gate, g), jnp.where(here, m, m_all), jnp.where(here, l, l_all)
    g_scr[...], m_scr[...], l_scr[...] = g, m_all, l_all
    for t, pb in enumerate(probs):
        a_scr[j * blocks_per_step + t] = jnp.concatenate(
            [_dot_nt(pb[c * rows:(c + 1) * rows], block_pages(v_refs, t, c)) for c in range(n_pairs)],
            axis=0)

    @pl.when(j == n_blk // blocks_per_step - 1)
    def _():
        lane = lane_id.astype(F32)
        g = jnp.where(lane < n_blk, g_scr[...], -jnp.inf)
        sel = jnp.zeros(g.shape, jnp.bool_)
        for _ in range(MOBA_TOPK):
            mx = jnp.max(g, axis=-1, keepdims=True)
            idx = jnp.min(jnp.where(g == mx, lane, float(LANES)), axis=-1, keepdims=True)
            pick = lane == idx
            sel = sel | pick
            g = jnp.where(pick, -jnp.inf, g)
        kn, vn = _pad_rows16(kn_ref[...]), _pad_rows16(vn_ref[...])
        qrow = lax.broadcasted_iota(jnp.int32, (rows, rows), 0) % n_tok
        kcol = lax.broadcasted_iota(jnp.int32, (rows, rows), 1)
        so = jnp.concatenate(
            [jnp.where(kcol <= qrow, _dot_nt(qp_scr[c, :rows, :], kn[:, c * LANES:(c + 1) * LANES]), NEG)
             for c in range(n_pairs)], axis=0)
        m_own = jnp.max(so, axis=-1, keepdims=True)
        m_sel = jnp.where(sel, m_scr[...], NEG)
        m_all = jnp.maximum(m_own, jnp.max(m_sel, axis=-1, keepdims=True))
        w = jnp.where(sel, jnp.exp(m_sel - m_all), 0.0)
        po = jnp.exp(so - m_all)
        l_all = jnp.sum(w * l_scr[...], axis=-1, keepdims=True) + jnp.sum(po, axis=-1, keepdims=True)
        pob = po.astype(BF16)
        acc = jnp.concatenate(
            [_dot(pob[c * rows:(c + 1) * rows], vn[:, c * LANES:(c + 1) * LANES]) for c in range(n_pairs)],
            axis=0)
        for n in range(n_blk):
            acc = acc + a_scr[n] * w[:, n:n + 1]
        acc = acc / l_all
        for c in range(n_pairs):
            o_ref[:, c * LANES:(c + 1) * LANES] = jnp.where(
                half == 0, acc[c * rows:c * rows + n_tok], acc[c * rows + n_tok:(c + 1) * rows])


def _moba_sample_part(page_table, q, k_new, v_new, k_cache, v_cache, *, layer, grid):
    n_seq, n_pages = page_table.shape
    n_tok = q.shape[0] // n_seq
    page = k_cache.shape[-1]
    per_blk = MOBA_BLOCK // page
    n_blk = n_pages // per_blk
    _, groups, n_steps = grid
    assert n_seq == grid[0] * groups and n_blk % n_steps == 0
    blocks_per_step = n_blk // n_steps
    assert n_tok * (D_MODEL // HEAD_DIM) == LANES and n_blk <= LANES
    per_step = per_blk * blocks_per_step
    row_spec = pl.BlockSpec((n_tok, D_MODEL), lambda bi, g, j, pt: (bi * groups + g, 0))

    def page_spec(i):
        return pl.BlockSpec((None, None, D_MODEL, page),
                            lambda bi, g, j, pt: (layer, pt[bi * groups + g, per_step * j + i], 0, 0))

    pages = [page_spec(i) for i in range(per_step)]
    return _Part(
        body=functools.partial(_moba_sample_kernel, n_blk=n_blk, blocks_per_step=blocks_per_step,
                               pages_per_block=per_blk),
        args=(q, k_new, v_new, *([k_cache] * per_step), *([v_cache] * per_step)),
        in_specs=[row_spec, row_spec, row_spec] + pages + pages,
        out_spec=row_spec,
        out_shape=jax.ShapeDtypeStruct(q.shape, F32),
        scratch=[pltpu.VMEM((D_MODEL // LANES, 4 * n_tok, LANES), BF16),
                 pltpu.VMEM((n_blk, LANES, LANES), F32),
                 pltpu.VMEM((LANES, LANES), F32),
                 pltpu.VMEM((LANES, LANES), F32),
                 pltpu.VMEM((LANES, LANES), F32)])


def _diff_sample_kernel(pt_ref, q_ref, kn_ref, vn_ref, lam_ref, gsub_ref, *refs,
                        pages_per_step, n_steps, lam_init):
    del pt_ref
    k_refs = refs[:pages_per_step]
    v_refs = refs[pages_per_step:2 * pages_per_step]
    o_ref, qd_scr, m_scr, l_scr, acc_scr = refs[2 * pages_per_step:]
    j = pl.program_id(2)
    n_tok = q_ref.shape[0]
    n_heads = D_MODEL // LANES
    page = k_refs[0].shape[0] // n_heads
    rows = 2 * n_tok
    half = _lane_half((1, LANES))

    def head_rows(page_refs, h):
        return jnp.concatenate([r[pl.ds(h, page, stride=n_heads), :] for r in page_refs], axis=0).astype(BF16)

    @pl.when(j == 0)
    def _():
        for h in range(n_heads):
            x = q_ref[:, h * LANES:(h + 1) * LANES] * SCALE
            qd_scr[h * rows:(h + 1) * rows, :] = jnp.concatenate(
                [jnp.where(half == 0, x, 0.0), jnp.where(half == 1, x, 0.0)], axis=0).astype(BF16)
        kn, vn = _pad_rows16(kn_ref[...]), _pad_rows16(vn_ref[...])
        qrow = lax.broadcasted_iota(jnp.int32, (rows, rows), 0) % n_tok
        kcol = lax.broadcasted_iota(jnp.int32, (rows, rows), 1)
        so = jnp.concatenate(
            [jnp.where(kcol <= qrow,
                       _dot_nt(qd_scr[h * rows:(h + 1) * rows, :], kn[:, h * LANES:(h + 1) * LANES]), NEG)
             for h in range(n_heads)], axis=0)
        m = jnp.max(so, axis=-1, keepdims=True)
        po = jnp.exp(so - m)
        pob = po.astype(BF16)
        m_scr[...] = m
        l_scr[...] = jnp.sum(po, axis=-1, keepdims=True)
        acc_scr[...] = jnp.concatenate(
            [_dot(pob[h * rows:(h + 1) * rows], vn[:, h * LANES:(h + 1) * LANES]) for h in range(n_heads)],
            axis=0)

    s = jnp.concatenate([_dot_nt(qd_scr[h * rows:(h + 1) * rows, :], head_rows(k_refs, h))
                         for h in range(n_heads)], axis=0)
    m = m_scr[...]
    m_new = jnp.maximum(m, jnp.max(s, axis=-1, keepdims=True))
    alpha = jnp.exp(m - m_new)
    p = jnp.exp(s - m_new)
    pb = p.astype(BF16)
    pv = jnp.concatenate([_dot(pb[h * rows:(h + 1) * rows], head_rows(v_refs, h)) for h in range(n_heads)],
                         axis=0)
    m_scr[...] = m_new
    l = alpha * l_scr[...] + jnp.sum(p, axis=-1, keepdims=True)
    acc = alpha * acc_scr[...] + pv
    l_scr[...] = l
    acc_scr[...] = acc

    @pl.when(j == n_steps - 1)
    def _():
        lam = _diff_lambda(lam_ref, lam_init)
        on = acc / l
        for h in range(n_heads):
            o = on[h * rows:h * rows + n_tok] - lam * on[h * rows + n_tok:(h + 1) * rows]
            ms = jnp.mean(o * o, axis=-1, keepdims=True)
            o_ref[:, h * LANES:(h + 1) * LANES] = o * lax.rsqrt(ms + EPS) * gsub_ref[...] * (1.0 - lam_init)


def _diff_sample_part(page_table, q, k_new, v_new, lam_vec, gsub_row, k_cache, v_cache, *, layer, lam_init, grid):
    n_seq, n_pages = page_table.shape
    n_tok = q.shape[0] // n_seq
    page_rows = k_cache.shape[2]
    _, groups, n_steps = grid
    assert n_seq == grid[0] * groups and n_pages % n_steps == 0
    pages_per_step = n_pages // n_steps
    assert 2 * n_tok * (D_MODEL // LANES) == LANES
    row_spec = pl.BlockSpec((n_tok, D_MODEL), lambda bi, g, j, pt: (bi * groups + g, 0))

    def page_spec(i):
        return pl.BlockSpec((None, None, page_rows, LANES),
                            lambda bi, g, j, pt: (layer, pt[bi * groups + g, pages_per_step * j + i], 0, 0))

    def const(shape):
        zeros = (0,) * len(shape)
        return pl.BlockSpec(shape, lambda *_: zeros)

    pages = [page_spec(i) for i in range(pages_per_step)]
    return _Part(
        body=functools.partial(_diff_sample_kernel, pages_per_step=pages_per_step, n_steps=n_steps,
                               lam_init=lam_init),
        args=(q, k_new, v_new, lam_vec, gsub_row, *([k_cache] * pages_per_step), *([v_cache] * pages_per_step)),
        in_specs=[row_spec, row_spec, row_spec, const((4, HEAD_DIM)), const((1, LANES))] + pages + pages,
        out_spec=row_spec,
        out_shape=jax.ShapeDtypeStruct(q.shape, F32),
        scratch=[pltpu.VMEM((LANES, LANES), BF16),
                 pltpu.VMEM((LANES, 1), F32),
                 pltpu.VMEM((LANES, 1), F32),
                 pltpu.VMEM((LANES, LANES), F32)])


def _fused_attn(page_table, prompt, sample, grid, name):
    n_pi, n_si = len(prompt.args), len(sample.args)
    n_ps = len(prompt.scratch)

    def body(pt_ref, *refs):
        p_in, s_in = refs[:n_pi], refs[n_pi:n_pi + n_si]
        p_out, s_out = refs[n_pi + n_si], refs[n_pi + n_si + 1]
        scr = refs[n_pi + n_si + 2:]
        prompt.body(*p_in, p_out, *scr[:n_ps])
        sample.body(pt_ref, *s_in, s_out, *scr[n_ps:])

    grid_spec = pltpu.PrefetchScalarGridSpec(
        num_scalar_prefetch=1,
        grid=grid,
        in_specs=prompt.in_specs + sample.in_specs,
        out_specs=[prompt.out_spec, sample.out_spec],
        scratch_shapes=prompt.scratch + sample.scratch)
    return pl.pallas_call(
        body,
        grid_spec=grid_spec,
        out_shape=[prompt.out_shape, sample.out_shape],
        compiler_params=_params(("parallel", "parallel", "arbitrary")),
        name=name,
    )(page_table, *prompt.args, *sample.args)


def kernel(x_prompt, x_sample, cache_moba_k, cache_moba_v, cache_diff_k, cache_diff_v, page_table, norm_mix, norm_mlp, moba_w_qkv, moba_w_o, moba_q_norm, moba_k_norm, diff_w_qkv, diff_w_o, diff_q_norm, diff_k_norm, diff_lambda, diff_subln, mlp_w_up, mlp_w_down):
    batch, seq, d = x_prompt.shape
    n_seq, n_tok, _ = x_sample.shape
    depth = norm_mix.shape[0]
    n_lay, n_phys, page, a_heads, _ = cache_moba_k.shape
    b_heads = cache_diff_k.shape[3]
    past_len = page_table.shape[1] * page
    tm = 256

    rope_p = _rope_tables(jnp.arange(seq))
    rope_s = _rope_tables(past_len + jnp.arange(n_seq * n_tok) % n_tok)

    moba_kc = jnp.transpose(cache_moba_k, (0, 1, 3, 4, 2)).reshape(n_lay, n_phys, d, page)
    moba_vc = jnp.transpose(cache_moba_v, (0, 1, 3, 4, 2)).reshape(n_lay, n_phys, d, page)
    diff_kc = cache_diff_k.reshape(cache_diff_k.shape[0], n_phys, page * b_heads, 2 * HEAD_DIM)
    diff_vc = cache_diff_v.reshape(cache_diff_v.shape[0], n_phys, page * b_heads, 2 * HEAD_DIM)

    def two_heads(gain):
        return jnp.tile(gain, LANES // HEAD_DIM).reshape(1, LANES)

    xp = x_prompt.reshape(batch * seq, d)
    xs = x_sample.reshape(n_seq * n_tok, d)
    grid = _prompt_grid(batch, seq, MOBA_BLOCK)
    kv_p = {True: None, False: None}
    kv_s = {True: None, False: None}
    for i in range(depth):
        j = i // N_MIXERS
        moba = i % N_MIXERS == 0
        n_slabs = (cache_moba_k if moba else cache_diff_k).shape[0]
        w_qkv = (moba_w_qkv if moba else diff_w_qkv)[j].astype(BF16)
        w_o = (moba_w_o if moba else diff_w_o)[j].astype(BF16)
        gq = two_heads((moba_q_norm if moba else diff_q_norm)[j])
        gk = two_heads((moba_k_norm if moba else diff_k_norm)[j])
        g_mix = norm_mix[i].reshape(1, d)
        qp, *kv_p[moba] = _qkv_proj(xp, g_mix, w_qkv, gq, gk, rope_p, seq_tiles=seq // tm, transpose_kv=moba,
                                    tm=tm, slab=j, n_slabs=n_slabs, prev=kv_p[moba])
        qs, *kv_s[moba] = _qkv_proj(xs, g_mix, w_qkv, gq, gk, rope_s, seq_tiles=1, transpose_kv=False,
                                    tm=n_seq * n_tok, slab=j, n_slabs=n_slabs, prev=kv_s[moba])
        ks, vs = kv_s[moba][0][j], kv_s[moba][1][j]
        if moba:
            prompt = _moba_prompt_part(qp, *kv_p[moba], layer=j)
            sample = _moba_sample_part(page_table, qs, ks, vs, moba_kc, moba_vc, layer=j, grid=grid)
            op, os_ = _fused_attn(page_table, prompt, sample, grid, "moba_attn")
        else:
            lam_init = 0.8 - 0.6 * math.exp(-0.3 * i)
            prompt = _diff_prompt_part(qp, *kv_p[moba], diff_lambda[j], diff_subln[j].reshape(LANES, 1),
                                       batch=batch, layer=j, lam_init=lam_init)
            sample = _diff_sample_part(page_table, qs, ks, vs, diff_lambda[j], diff_subln[j].reshape(1, LANES),
                                       diff_kc, diff_vc, layer=j, lam_init=lam_init, grid=grid)
            op, os_ = _fused_attn(page_table, prompt, sample, grid, "diff_attn")
            op = op.reshape(batch * seq, d)
        g_mlp = norm_mlp[i].reshape(1, d)
        w_up, w_dn = mlp_w_up[i].astype(BF16), mlp_w_down[i].astype(BF16)
        xp = _out_mlp(xp, op, w_o, g_mlp, w_up, w_dn, tm=tm)
        xs = _out_mlp(xs, os_, w_o, g_mlp, w_up, w_dn, tm=n_seq * n_tok)

    def moba_prompt_out(stack):
        return jnp.transpose(stack.reshape(stack.shape[0], batch, a_heads, HEAD_DIM, seq), (0, 1, 4, 2, 3))

    def rows_out(stack, n_b, n_s, heads):
        return stack.reshape(stack.shape[0], n_b, n_s, heads, d // heads)

    return (xp.reshape(batch, seq, d), xs.reshape(n_seq, n_tok, d),
            moba_prompt_out(kv_p[True][0]), moba_prompt_out(kv_p[True][1]),
            rows_out(kv_p[False][0], batch, seq, b_heads), rows_out(kv_p[False][1], batch, seq, b_heads),
            rows_out(kv_s[True][0], n_seq, n_tok, a_heads), rows_out(kv_s[True][1], n_seq, n_tok, a_heads),
            rows_out(kv_s[False][0], n_seq, n_tok, b_heads), rows_out(kv_s[False][1], n_seq, n_tok, b_heads))
```

```python
import functools
import math
from typing import Any, NamedTuple

import jax
import jax.numpy as jnp
from jax import lax
from jax.experimental import pallas as pl
from jax.experimental.pallas import tpu as pltpu

F32 = jnp.float32
BF16 = jnp.bfloat16

D_MODEL = 1024
HEAD_DIM = 64
ROT_DIM = HEAD_DIM // 4
ROPE_THETA = 500000.0
MOBA_BLOCK = 256
MOBA_TOPK = 3
Q_CHUNK_DIFF = 128
EPS = 1e-6
N_MIXERS = 2
LANES = 128
BF16_ROWS = 16
SCALE = HEAD_DIM ** -0.5
LOG2E = math.log2(math.e)
PROMPT_STREAMS = 4
STREAM_BUFS = 2
NEG = -1e30
VMEM_LIMIT = 56 * 1024 * 1024

_NT = (((1,), (1,)), ((), ()))


def _dot(a, b):
    return jnp.dot(a, b, preferred_element_type=F32)


def _dot_nt(a, b):
    return lax.dot_general(a, b, _NT, preferred_element_type=F32)


def _split_bf16(x):
    hi = x.astype(BF16)
    lo = (x - hi.astype(F32)).astype(BF16)
    return hi, lo


def _params(sem):
    return pltpu.CompilerParams(dimension_semantics=sem, vmem_limit_bytes=VMEM_LIMIT)


def _const_spec(shape):
    zeros = (0,) * len(shape)
    return pl.BlockSpec(shape, lambda *_: zeros, pipeline_mode=pl.Buffered(1))


def _lane_half(shape):
    return lax.broadcasted_iota(jnp.int32, shape, len(shape) - 1) // HEAD_DIM


def _qkv_kernel(x_ref, g_ref, w_ref, gq_ref, gk_ref, cos_ref, sa_ref, sb_ref, *refs, transpose_kv):
    q_ref, k_ref, v_ref = refs[-3:]
    x = x_ref[...]
    ms = jnp.mean(x * x, axis=-1, keepdims=True)
    h = (x * lax.rsqrt(ms + EPS) * g_ref[...]).astype(BF16)
    y = _dot(h, w_ref[...])
    cos, sa, sb = cos_ref[...], sa_ref[...], sb_ref[...]
    first = _lane_half((1, LANES)) == 0

    def norm_rope(blk, gain):
        sq = blk * blk
        s0 = jnp.sum(jnp.where(first, sq, 0.0), axis=-1, keepdims=True)
        s1 = jnp.sum(jnp.where(first, 0.0, sq), axis=-1, keepdims=True)
        r = lax.rsqrt(jnp.where(first, s0, s1) * (1.0 / HEAD_DIM) + EPS)
        n = blk * r * gain
        return n * cos + pltpu.roll(n, LANES - ROT_DIM // 2, 1) * sa + pltpu.roll(n, ROT_DIM // 2, 1) * sb

    n_tiles = D_MODEL // LANES
    for c in range(n_tiles):
        sl = slice(c * LANES, (c + 1) * LANES)
        q_ref[:, sl] = norm_rope(y[:, sl], gq_ref[...])
        kb = norm_rope(y[:, D_MODEL + c * LANES:D_MODEL + (c + 1) * LANES], gk_ref[...])
        vb = y[:, 2 * D_MODEL + c * LANES:2 * D_MODEL + (c + 1) * LANES]
        if transpose_kv:
            k_ref[sl, :] = kb.T
            v_ref[sl, :] = vb.T
        else:
            k_ref[:, sl] = kb
            v_ref[:, sl] = vb


def _qkv_proj(x, g, w, gq, gk, rope, *, seq_tiles, transpose_kv, tm, slab, n_slabs, prev=None):
    t = x.shape[0]
    n_tiles = t // tm
    cos, sa, sb = rope
    row_spec = pl.BlockSpec((tm, D_MODEL), lambda i: (i, 0))
    rope_spec = pl.BlockSpec((tm, LANES), lambda i: (i % seq_tiles, 0))
    if transpose_kv:
        kv_shape = jax.ShapeDtypeStruct((n_slabs, n_tiles // seq_tiles, D_MODEL, seq_tiles * tm), F32)
        kv_spec = pl.BlockSpec((None, None, D_MODEL, tm), lambda i: (slab, i // seq_tiles, 0, i % seq_tiles))
    else:
        kv_shape = jax.ShapeDtypeStruct((n_slabs, t, D_MODEL), F32)
        kv_spec = pl.BlockSpec((None, tm, D_MODEL), lambda i: (slab, i, 0))
    in_specs = [row_spec, _const_spec((1, D_MODEL)), _const_spec((D_MODEL, 3 * D_MODEL)),
                _const_spec((1, LANES)), _const_spec((1, LANES)), rope_spec, rope_spec, rope_spec]
    args = (x, g, w, gq, gk, cos, sa, sb)
    aliases = {}
    if prev is not None:
        aliases = {len(args): 1, len(args) + 1: 2}
        in_specs = in_specs + [pl.BlockSpec(memory_space=pl.ANY)] * 2
        args = args + tuple(prev)
    return pl.pallas_call(
        functools.partial(_qkv_kernel, transpose_kv=transpose_kv),
        grid=(n_tiles,),
        in_specs=in_specs,
        out_specs=[row_spec, kv_spec, kv_spec],
        out_shape=[jax.ShapeDtypeStruct((t, D_MODEL), F32), kv_shape, kv_shape],
        input_output_aliases=aliases,
        compiler_params=_params(("parallel",)),
        name="qkv_proj_t" if transpose_kv else "qkv_proj",
    )(*args)


def _rope_tables(pos):
    half = ROT_DIM // 2
    inv_freq = ROPE_THETA ** (-jnp.arange(half, dtype=F32) * (2.0 / ROT_DIM))
    ang = pos.astype(F32)[:, None] * inv_freq[None, :]
    lane = jnp.arange(LANES)
    d = lane % HEAD_DIM
    cos_l = jnp.cos(ang)[:, lane % half]
    sin_l = jnp.sin(ang)[:, lane % half]
    cos = jnp.where(d < ROT_DIM, cos_l, 1.0)
    sa = jnp.where(d < half, -sin_l, 0.0)
    sb = jnp.where((d >= half) & (d < ROT_DIM), sin_l, 0.0)
    return cos.astype(F32), sa.astype(F32), sb.astype(F32)


def _mlp_kernel(x_ref, o_ref, wo_ref, g_ref, wup_ref, wdn_ref, out_ref, *, ff_chunk):
    x1 = x_ref[...] + _dot(o_ref[...].astype(BF16), wo_ref[...])
    ms = jnp.mean(x1 * x1, axis=-1, keepdims=True)
    h = (x1 * lax.rsqrt(ms + EPS) * g_ref[...]).astype(BF16)
    acc = x1
    for c in range(wup_ref.shape[1] // ff_chunk):
        u = jnp.maximum(_dot(h, wup_ref[:, c * ff_chunk:(c + 1) * ff_chunk]), 0.0)
        acc = acc + _dot((u * u).astype(BF16), wdn_ref[c * ff_chunk:(c + 1) * ff_chunk, :])
    out_ref[...] = acc


def _out_mlp(x, o, wo, g, wup, wdn, *, tm):
    t = x.shape[0]
    d_ff = wup.shape[1]
    row_spec = pl.BlockSpec((tm, D_MODEL), lambda i: (i, 0))
    return pl.pallas_call(
        functools.partial(_mlp_kernel, ff_chunk=1024),
        grid=(t // tm,),
        in_specs=[row_spec, row_spec, _const_spec((D_MODEL, D_MODEL)), _const_spec((1, D_MODEL)),
                  _const_spec((D_MODEL, d_ff)), _const_spec((d_ff, D_MODEL))],
        out_specs=row_spec,
        out_shape=jax.ShapeDtypeStruct((t, D_MODEL), F32),
        compiler_params=_params(("parallel",)),
        name="out_mlp",
    )(x, o, wo, g, wup, wdn)


def _causal_mask(tq):
    key = lax.broadcasted_iota(jnp.int32, (tq, tq), 0)
    qry = lax.broadcasted_iota(jnp.int32, (tq, tq), 1)
    return key <= qry


def _softmax_stream(k_scr, qb, s_scr, p_scr, n_blocks, bias, causal, finish):
    blk = k_scr.shape[1]
    st = {}

    def score(n):
        s = _dot_nt(k_scr[n], qb)
        if n == n_blocks - 1:
            s = jnp.where(causal, s, NEG)
        elif bias is not None:
            s = s + bias[n]
        s_scr[n] = s
        cm = jnp.max(s, axis=0, keepdims=True)
        st["m"] = cm if n == 0 else jnp.maximum(st["m"], cm)

    def prob(n):
        p = jnp.exp2(s_scr[n] - st["m"])
        cs = jnp.sum(p, axis=0, keepdims=True)
        st["l"] = cs if n == 0 else st["l"] + cs
        p_scr[n * blk:(n + 1) * blk, :] = p.astype(BF16)

    stages = [functools.partial(score, n) for n in range(n_blocks)]
    stages += [functools.partial(prob, n) for n in range(n_blocks)]
    return stages + [lambda: finish(st["l"])]


def _run_staggered(streams, lag):
    total = max(i * lag + len(s) for i, s in enumerate(streams))
    for t in range(total):
        for i, s in enumerate(streams):
            if 0 <= t - i * lag < len(s):
                s[t - i * lag]()


def _moba_bias(km_scr, qa, n_past):
    km = km_scr[...]
    n_blk = km.shape[0]
    pad = -n_blk % BF16_ROWS
    kmh, kml = _split_bf16(jnp.concatenate([km, jnp.zeros((pad, km.shape[1]), F32)], axis=0))
    qh, ql = _split_bf16(qa)
    gate = (_dot_nt(kmh, qh) + _dot_nt(kmh, ql) + _dot_nt(kml, qh))[:n_blk]
    blk_id = lax.broadcasted_iota(jnp.int32, (n_blk, 1), 0)
    past = blk_id < n_past
    gate = jnp.where(past, gate, -jnp.inf)
    bias = []
    for n in range(n_past):
        gn = gate[n:n + 1, :]
        beats = past & ((gate > gn) | ((gate == gn) & (blk_id < n)))
        rank = jnp.sum(beats.astype(F32), axis=0, keepdims=True)
        bias.append(jnp.where(rank < MOBA_TOPK, 0.0, NEG))
    return bias


def _moba_prompt_kernel(q_ref, kt_ref, vt_ref, o_ref, k_scr, vt_scr, km_scr, s_scr, p_scr, *, n_blk):
    qi = pl.program_id(2)
    blk = MOBA_BLOCK
    n_pairs = q_ref.shape[1] // LANES

    @pl.when(qi == 0)
    def _():
        for c in range(n_pairs):
            for n in range(n_blk):
                kn = kt_ref[c * LANES:(c + 1) * LANES, n * blk:(n + 1) * blk].T
                km_scr[c, n:n + 1, :] = jnp.mean(kn, axis=0, keepdims=True)
                k_scr[c, n] = kn.astype(BF16)
        vt_scr[...] = vt_ref[...].astype(BF16)

    def tile(k):
        half = _lane_half((1, LANES))
        causal = _causal_mask(blk)
        n_keys = (k + 1) * blk
        outs = [None] * (2 * n_pairs)
        streams = []
        for c in range(n_pairs):
            q = q_ref[:, c * LANES:(c + 1) * LANES]
            for a in range(2):
                i = 2 * c + a
                qa = jnp.where(half == a, q, 0.0)
                bias = _moba_bias(km_scr.at[c], qa, k) if k > MOBA_TOPK else None
                qb = (qa * (SCALE * LOG2E)).astype(BF16)

                def finish(l, i=i):
                    acc = _dot(vt_scr[i * HEAD_DIM:(i + 1) * HEAD_DIM, :n_keys], p_scr[i % STREAM_BUFS, :n_keys, :])
                    outs[i] = acc / l

                streams.append(_softmax_stream(k_scr.at[c], qb, s_scr.at[i % STREAM_BUFS], p_scr.at[i % STREAM_BUFS], k + 1, bias,
                                               causal, finish))
        _run_staggered(streams, lag=k + 1)
        for c in range(n_pairs):
            o_ref[:, c * LANES:(c + 1) * LANES] = jnp.concatenate(
                outs[2 * c:2 * c + 2], axis=0).T.astype(o_ref.dtype)

    for k in range(n_blk):
        pl.when(qi == k)(functools.partial(tile, k))


class _Part(NamedTuple):
    body: Any
    args: tuple
    in_specs: list
    out_spec: Any
    out_shape: Any
    scratch: list


def _prompt_grid(batch, s, tq):
    return (batch, D_MODEL // (PROMPT_STREAMS // 2 * LANES), s // tq)


def _moba_prompt_part(q, kt, vt, *, layer):
    _, b, _, s = kt.shape
    n_blk = s // MOBA_BLOCK
    tq = MOBA_BLOCK
    width = PROMPT_STREAMS // 2 * LANES
    n_streams = PROMPT_STREAMS
    q_spec = pl.BlockSpec((tq, width), lambda bi, hp, st, pt: (bi * n_blk + st, hp))
    kv_spec = pl.BlockSpec((None, None, width, s), lambda bi, hp, st, pt: (layer, bi, hp, 0))
    return _Part(
        body=functools.partial(_moba_prompt_kernel, n_blk=n_blk),
        args=(q, kt, vt),
        in_specs=[q_spec, kv_spec, kv_spec],
        out_spec=q_spec,
        out_shape=jax.ShapeDtypeStruct((b * s, D_MODEL), BF16),
        scratch=[pltpu.VMEM((width // LANES, n_blk, MOBA_BLOCK, LANES), BF16),
                 pltpu.VMEM((width, s), BF16),
                 pltpu.VMEM((width // LANES, n_blk, LANES), F32),
                 pltpu.VMEM((STREAM_BUFS, n_blk, MOBA_BLOCK, tq), F32),
                 pltpu.VMEM((STREAM_BUFS, s, tq), BF16)])


def _diff_lambda(lam_ref, lam_init):
    lv = lam_ref[...]
    a = jnp.sum(lv[0:1] * lv[1:2], axis=-1, keepdims=True)
    b = jnp.sum(lv[2:3] * lv[3:4], axis=-1, keepdims=True)
    return jnp.exp(a) - jnp.exp(b) + lam_init


def _diff_prompt_kernel(q_ref, k_ref, v_ref, lam_ref, gsub_ref, o_ref, k_scr, vt_scr, s_scr, p_scr,
                        *, n_blk, lam_init):
    qi = pl.program_id(2)
    tq = q_ref.shape[0]
    n_heads = q_ref.shape[1] // LANES

    @pl.when(qi == 0)
    def _():
        for h in range(n_heads):
            for n in range(n_blk):
                k_scr[h, n] = k_ref[n * tq:(n + 1) * tq, h * LANES:(h + 1) * LANES].astype(BF16)
                vt_scr[h * LANES:(h + 1) * LANES, n * tq:(n + 1) * tq] = (
                    v_ref[n * tq:(n + 1) * tq, h * LANES:(h + 1) * LANES].T.astype(BF16))

    def tile(k):
        half = _lane_half((1, LANES))
        causal = _causal_mask(tq)
        n_keys = (k + 1) * tq
        outs = [None] * (2 * n_heads)
        streams = []
        for h in range(n_heads):
            q = q_ref[:, h * LANES:(h + 1) * LANES]
            for c in range(2):
                i = 2 * h + c
                qb = (jnp.where(half == c, q, 0.0) * (SCALE * LOG2E)).astype(BF16)

                def finish(l, i=i, h=h):
                    outs[i] = _dot(vt_scr[h * LANES:(h + 1) * LANES, :n_keys], p_scr[i % STREAM_BUFS, :n_keys, :]) / l

                streams.append(_softmax_stream(k_scr.at[h], qb, s_scr.at[i % STREAM_BUFS], p_scr.at[i % STREAM_BUFS], k + 1, None,
                                               causal, finish))
        _run_staggered(streams, lag=k + 1)
        lam = _diff_lambda(lam_ref, lam_init)
        for h in range(n_heads):
            ot = outs[2 * h] - lam * outs[2 * h + 1]
            ms = jnp.mean(ot * ot, axis=0, keepdims=True)
            ot = ot * lax.rsqrt(ms + EPS) * gsub_ref[...] * (1.0 - lam_init)
            o = ot.T.astype(o_ref.dtype)
            for c in range(o_ref.shape[0]):
                o_ref[c, :, h * LANES:(h + 1) * LANES] = o[c * Q_CHUNK_DIFF:(c + 1) * Q_CHUNK_DIFF]

    for k in range(n_blk):
        pl.when(qi == k)(functools.partial(tile, k))


def _diff_prompt_part(q, k, v, lam_vec, gsub_col, *, batch, layer, lam_init):
    t = q.shape[0]
    s = t // batch
    tq = MOBA_BLOCK
    assert s % tq == 0 and tq % Q_CHUNK_DIFF == 0
    n_blk = s // tq
    n_streams = PROMPT_STREAMS
    width = n_streams // 2 * LANES
    q_spec = pl.BlockSpec((tq, width), lambda bi, h, st, pt: (bi * n_blk + st, h))
    kv_spec = pl.BlockSpec((None, s, width), lambda bi, h, st, pt: (layer, bi, h))
    per_tile = tq // Q_CHUNK_DIFF
    o_spec = pl.BlockSpec((per_tile, None, Q_CHUNK_DIFF, width),
                          lambda bi, h, st, pt: (st, bi, 0, h))
    return _Part(
        body=functools.partial(_diff_prompt_kernel, n_blk=n_blk, lam_init=lam_init),
        args=(q, k, v, lam_vec, gsub_col),
        in_specs=[q_spec, kv_spec, kv_spec, _const_spec((4, HEAD_DIM)), _const_spec((LANES, 1))],
        out_spec=o_spec,
        out_shape=jax.ShapeDtypeStruct((s // Q_CHUNK_DIFF, batch, Q_CHUNK_DIFF, D_MODEL), BF16),
        scratch=[pltpu.VMEM((width // LANES, n_blk, tq, LANES), BF16),
                 pltpu.VMEM((width, s), BF16),
                 pltpu.VMEM((STREAM_BUFS, n_blk, tq, tq), F32),
                 pltpu.VMEM((STREAM_BUFS, s, tq), BF16)])


def _pad_rows16(x):
    return jnp.concatenate([x, jnp.zeros_like(x)], axis=0).astype(BF16)


def _moba_sample_kernel(pt_ref, q_ref, kn_ref, vn_ref, *refs, n_blk, blocks_per_step, pages_per_block):
    del pt_ref
    n_pages = blocks_per_step * pages_per_block
    k_refs, v_refs = refs[:n_pages], refs[n_pages:2 * n_pages]
    o_ref, qp_scr, a_scr, g_scr, m_scr, l_scr = refs[2 * n_pages:]
    j = pl.program_id(2)
    n_tok = q_ref.shape[0]
    n_pairs = D_MODEL // LANES
    rows = 2 * n_tok
    half = _lane_half((1, LANES))

    @pl.when(j == 0)
    def _():
        for c in range(n_pairs):
            x = q_ref[:, c * LANES:(c + 1) * LANES] * SCALE
            qp = jnp.concatenate([jnp.where(half == 0, x, 0.0), jnp.where(half == 1, x, 0.0)], axis=0)
            hi, lo = _split_bf16(qp)
            qp_scr[c] = jnp.concatenate([hi, lo], axis=0)
        g_scr[...] = jnp.zeros_like(g_scr)
        m_scr[...] = jnp.zeros_like(m_scr)
        l_scr[...] = jnp.zeros_like(l_scr)

    lane_id = lax.broadcasted_iota(jnp.int32, (1, LANES), 1)
    def block_pages(refs, t, c):
        pages = range(t * pages_per_block, (t + 1) * pages_per_block)
        return jnp.concatenate([refs[i][c * LANES:(c + 1) * LANES, :] for i in pages], axis=1).astype(BF16)

    scores = []
    for t in range(blocks_per_step):
        parts = []
        for c in range(n_pairs):
            s2 = _dot(qp_scr[c], block_pages(k_refs, t, c))
            parts.append(s2[:rows] + s2[rows:])
        scores.append(jnp.concatenate(parts, axis=0))
    g, m_all, l_all = g_scr[...], m_scr[...], l_scr[...]
    probs = []
    for t, s in enumerate(scores):
        gate = jnp.mean(s, axis=-1, keepdims=True)
        m = jnp.max(s, axis=-1, keepdims=True)
        p = jnp.exp(s - m)
        l = jnp.sum(p, axis=-1, keepdims=True)
        probs.append(p.astype(BF16))
        here = lane_id == j * blocks_per_step + t
        g, m_all, l_all = jnp.where(here, gate, g), jnp.where(here, m, m_all), jnp.where(here, l, l_all)
    g_scr[...], m_scr[...], l_scr[...] = g, m_all, l_all
    for t, pb in enumerate(probs):
        a_scr[j * blocks_per_step + t] = jnp.concatenate(
            [_dot_nt(pb[c * rows:(c + 1) * rows], block_pages(v_refs, t, c)) for c in range(n_pairs)],
            axis=0)

    @pl.when(j == n_blk // blocks_per_step - 1)
    def _():
        lane = lane_id.astype(F32)
        g = jnp.where(lane < n_blk, g_scr[...], -jnp.inf)
        sel = jnp.zeros(g.shape, jnp.bool_)
        for _ in range(MOBA_TOPK):
            mx = jnp.max(g, axis=-1, keepdims=True)
            idx = jnp.min(jnp.where(g == mx, lane, float(LANES)), axis=-1, keepdims=True)
            pick = lane == idx
            sel = sel | pick
            g = jnp.where(pick, -jnp.inf, g)
        kn, vn = _pad_rows16(kn_ref[...]), _pad_rows16(vn_ref[...])
        qrow = lax.broadcasted_iota(jnp.int32, (rows, rows), 0) % n_tok
        kcol = lax.broadcasted_iota(jnp.int32, (rows, rows), 1)
        so = jnp.concatenate(
            [jnp.where(kcol <= qrow, _dot_nt(qp_scr[c, :rows, :], kn[:, c * LANES:(c + 1) * LANES]), NEG)
             for c in range(n_pairs)], axis=0)
        m_own = jnp.max(so, axis=-1, keepdims=True)
        m_sel = jnp.where(sel, m_scr[...], NEG)
        m_all = jnp.maximum(m_own, jnp.max(m_sel, axis=-1, keepdims=True))
        w = jnp.where(sel, jnp.exp(m_sel - m_all), 0.0)
        po = jnp.exp(so - m_all)
        l_all = jnp.sum(w * l_scr[...], axis=-1, keepdims=True) + jnp.sum(po, axis=-1, keepdims=True)
        pob = po.astype(BF16)
        acc = jnp.concatenate(
            [_dot(pob[c * rows:(c + 1) * rows], vn[:, c * LANES:(c + 1) * LANES]) for c in range(n_pairs)],
            axis=0)
        for n in range(n_blk):
            acc = acc + a_scr[n] * w[:, n:n + 1]
        acc = acc / l_all
        for c in range(n_pairs):
            o_ref[:, c * LANES:(c + 1) * LANES] = jnp.where(
                half == 0, acc[c * rows:c * rows + n_tok], acc[c * rows + n_tok:(c + 1) * rows])


def _moba_sample_part(page_table, q, k_new, v_new, k_cache, v_cache, *, layer, grid):
    n_seq, n_pages = page_table.shape
    n_tok = q.shape[0] // n_seq
    page = k_cache.shape[-1]
    per_blk = MOBA_BLOCK // page
    n_blk = n_pages // per_blk
    _, groups, n_steps = grid
    assert n_seq == grid[0] * groups and n_blk % n_steps == 0
    blocks_per_step = n_blk // n_steps
    assert n_tok * (D_MODEL // HEAD_DIM) == LANES and n_blk <= LANES
    per_step = per_blk * blocks_per_step
    row_spec = pl.BlockSpec((n_tok, D_MODEL), lambda bi, g, j, pt: (bi * groups + g, 0))

    def page_spec(i):
        return pl.BlockSpec((None, None, D_MODEL, page),
                            lambda bi, g, j, pt: (layer, pt[bi * groups + g, per_step * j + i], 0, 0))

    pages = [page_spec(i) for i in range(per_step)]
    return _Part(
        body=functools.partial(_moba_sample_kernel, n_blk=n_blk, blocks_per_step=blocks_per_step,
                               pages_per_block=per_blk),
        args=(q, k_new, v_new, *([k_cache] * per_step), *([v_cache] * per_step)),
        in_specs=[row_spec, row_spec, row_spec] + pages + pages,
        out_spec=row_spec,
        out_shape=jax.ShapeDtypeStruct(q.shape, F32),
        scratch=[pltpu.VMEM((D_MODEL // LANES, 4 * n_tok, LANES), BF16),
                 pltpu.VMEM((n_blk, LANES, LANES), F32),
                 pltpu.VMEM((LANES, LANES), F32),
                 pltpu.VMEM((LANES, LANES), F32),
                 pltpu.VMEM((LANES, LANES), F32)])


def _diff_sample_kernel(pt_ref, q_ref, kn_ref, vn_ref, lam_ref, gsub_ref, *refs,
                        pages_per_step, n_steps, lam_init):
    del pt_ref
    k_refs = refs[:pages_per_step]
    v_refs = refs[pages_per_step:2 * pages_per_step]
    o_ref, qd_scr, m_scr, l_scr, acc_scr = refs[2 * pages_per_step:]
    j = pl.program_id(2)
    n_tok = q_ref.shape[0]
    n_heads = D_MODEL // LANES
    page = k_refs[0].shape[0] // n_heads
    rows = 2 * n_tok
    half = _lane_half((1, LANES))

    def head_rows(page_refs, h):
        return jnp.concatenate([r[pl.ds(h, page, stride=n_heads), :] for r in page_refs], axis=0).astype(BF16)

    @pl.when(j == 0)
    def _():
        for h in range(n_heads):
            x = q_ref[:, h * LANES:(h + 1) * LANES] * SCALE
            qd_scr[h * rows:(h + 1) * rows, :] = jnp.concatenate(
                [jnp.where(half == 0, x, 0.0), jnp.where(half == 1, x, 0.0)], axis=0).astype(BF16)
        kn, vn = _pad_rows16(kn_ref[...]), _pad_rows16(vn_ref[...])
        qrow = lax.broadcasted_iota(jnp.int32, (rows, rows), 0) % n_tok
        kcol = lax.broadcasted_iota(jnp.int32, (rows, rows), 1)
        so = jnp.concatenate(
            [jnp.where(kcol <= qrow,
                       _dot_nt(qd_scr[h * rows:(h + 1) * rows, :], kn[:, h * LANES:(h + 1) * LANES]), NEG)
             for h in range(n_heads)], axis=0)
        m = jnp.max(so, axis=-1, keepdims=True)
        po = jnp.exp(so - m)
        pob = po.astype(BF16)
        m_scr[...] = m
        l_scr[...] = jnp.sum(po, axis=-1, keepdims=True)
        acc_scr[...] = jnp.concatenate(
            [_dot(pob[h * rows:(h + 1) * rows], vn[:, h * LANES:(h + 1) * LANES]) for h in range(n_heads)],
            axis=0)

    s = jnp.concatenate([_dot_nt(qd_scr[h * rows:(h + 1) * rows, :], head_rows(k_refs, h))
                         for h in range(n_heads)], axis=0)
    m = m_scr[...]
    m_new = jnp.maximum(m, jnp.max(s, axis=-1, keepdims=True))
    alpha = jnp.exp(m - m_new)
    p = jnp.exp(s - m_new)
    pb = p.astype(BF16)
    pv = jnp.concatenate([_dot(pb[h * rows:(h + 1) * rows], head_rows(v_refs, h)) for h in range(n_heads)],
                         axis=0)
    m_scr[...] = m_new
    l = alpha * l_scr[...] + jnp.sum(p, axis=-1, keepdims=True)
    acc = alpha * acc_scr[...] + pv
    l_scr[...] = l
    acc_scr[...] = acc

    @pl.when(j == n_steps - 1)
    def _():
        lam = _diff_lambda(lam_ref, lam_init)
        on = acc / l
        for h in range(n_heads):
            o = on[h * rows:h * rows + n_tok] - lam * on[h * rows + n_tok:(h + 1) * rows]
            ms = jnp.mean(o * o, axis=-1, keepdims=True)
            o_ref[:, h * LANES:(h + 1) * LANES] = o * lax.rsqrt(ms + EPS) * gsub_ref[...] * (1.0 - lam_init)


def _diff_sample_part(page_table, q, k_new, v_new, lam_vec, gsub_row, k_cache, v_cache, *, layer, lam_init, grid):
    n_seq, n_pages = page_table.shape
    n_tok = q.shape[0] // n_seq
    page_rows = k_cache.shape[2]
    _, groups, n_steps = grid
    assert n_seq == grid[0] * groups and n_pages % n_steps == 0
    pages_per_step = n_pages // n_steps
    assert 2 * n_tok * (D_MODEL // LANES) == LANES
    row_spec = pl.BlockSpec((n_tok, D_MODEL), lambda bi, g, j, pt: (bi * groups + g, 0))

    def page_spec(i):
        return pl.BlockSpec((None, None, page_rows, LANES),
                            lambda bi, g, j, pt: (layer, pt[bi * groups + g, pages_per_step * j + i], 0, 0))

    def const(shape):
        zeros = (0,) * len(shape)
        return pl.BlockSpec(shape, lambda *_: zeros)

    pages = [page_spec(i) for i in range(pages_per_step)]
    return _Part(
        body=functools.partial(_diff_sample_kernel, pages_per_step=pages_per_step, n_steps=n_steps,
                               lam_init=lam_init),
        args=(q, k_new, v_new, lam_vec, gsub_row, *([k_cache] * pages_per_step), *([v_cache] * pages_per_step)),
        in_specs=[row_spec, row_spec, row_spec, const((4, HEAD_DIM)), const((1, LANES))] + pages + pages,
        out_spec=row_spec,
        out_shape=jax.ShapeDtypeStruct(q.shape, F32),
        scratch=[pltpu.VMEM((LANES, LANES), BF16),
                 pltpu.VMEM((LANES, 1), F32),
                 pltpu.VMEM((LANES, 1), F32),
                 pltpu.VMEM((LANES, LANES), F32)])


def _fused_attn(page_table, prompt, sample, grid, name):
    n_pi, n_si = len(prompt.args), len(sample.args)
    n_ps = len(prompt.scratch)

    def body(pt_ref, *refs):
        p_in, s_in = refs[:n_pi], refs[n_pi:n_pi + n_si]
        p_out, s_out = refs[n_pi + n_si], refs[n_pi + n_si + 1]
        scr = refs[n_pi + n_si + 2:]
        prompt.body(*p_in, p_out, *scr[:n_ps])
        sample.body(pt_ref, *s_in, s_out, *scr[n_ps:])

    grid_spec = pltpu.PrefetchScalarGridSpec(
        num_scalar_prefetch=1,
        grid=grid,
        in_specs=prompt.in_specs + sample.in_specs,
        out_specs=[prompt.out_spec, sample.out_spec],
        scratch_shapes=prompt.scratch + sample.scratch)
    return pl.pallas_call(
        body,
        grid_spec=grid_spec,
        out_shape=[prompt.out_shape, sample.out_shape],
        compiler_params=_params(("parallel", "parallel", "arbitrary")),
        name=name,
    )(page_table, *prompt.args, *sample.args)


def kernel(x_prompt, x_sample, cache_moba_k, cache_moba_v, cache_diff_k, cache_diff_v, page_table, norm_mix, norm_mlp, moba_w_qkv, moba_w_o, moba_q_norm, moba_k_norm, diff_w_qkv, diff_w_o, diff_q_norm, diff_k_norm, diff_lambda, diff_subln, mlp_w_up, mlp_w_down):
    batch, seq, d = x_prompt.shape
    n_seq, n_tok, _ = x_sample.shape
    depth = norm_mix.shape[0]
    n_lay, n_phys, page, a_heads, _ = cache_moba_k.shape
    b_heads = cache_diff_k.shape[3]
    past_len = page_table.shape[1] * page
    tm = 512

    rope_p = _rope_tables(jnp.arange(seq))
    rope_s = _rope_tables(past_len + jnp.arange(n_seq * n_tok) % n_tok)

    moba_kc = jnp.transpose(cache_moba_k, (0, 1, 3, 4, 2)).reshape(n_lay, n_phys, d, page)
    moba_vc = jnp.transpose(cache_moba_v, (0, 1, 3, 4, 2)).reshape(n_lay, n_phys, d, page)
    diff_kc = cache_diff_k.reshape(cache_diff_k.shape[0], n_phys, page * b_heads, 2 * HEAD_DIM)
    diff_vc = cache_diff_v.reshape(cache_diff_v.shape[0], n_phys, page * b_heads, 2 * HEAD_DIM)

    def two_heads(gain):
        return jnp.tile(gain, LANES // HEAD_DIM).reshape(1, LANES)

    xp = x_prompt.reshape(batch * seq, d)
    xs = x_sample.reshape(n_seq * n_tok, d)
    grid = _prompt_grid(batch, seq, MOBA_BLOCK)
    kv_p = {True: None, False: None}
    kv_s = {True: None, False: None}
    for i in range(depth):
        j = i // N_MIXERS
        moba = i % N_MIXERS == 0
        n_slabs = (cache_moba_k if moba else cache_diff_k).shape[0]
        w_qkv = (moba_w_qkv if moba else diff_w_qkv)[j].astype(BF16)
        w_o = (moba_w_o if moba else diff_w_o)[j].astype(BF16)
        gq = two_heads((moba_q_norm if moba else diff_q_norm)[j])
        gk = two_heads((moba_k_norm if moba else diff_k_norm)[j])
        g_mix = norm_mix[i].reshape(1, d)
        qp, *kv_p[moba] = _qkv_proj(xp, g_mix, w_qkv, gq, gk, rope_p, seq_tiles=seq // tm, transpose_kv=moba,
                                    tm=tm, slab=j, n_slabs=n_slabs, prev=kv_p[moba])
        qs, *kv_s[moba] = _qkv_proj(xs, g_mix, w_qkv, gq, gk, rope_s, seq_tiles=1, transpose_kv=False,
                                    tm=n_seq * n_tok, slab=j, n_slabs=n_slabs, prev=kv_s[moba])
        ks, vs = kv_s[moba][0][j], kv_s[moba][1][j]
        if moba:
            prompt = _moba_prompt_part(qp, *kv_p[moba], layer=j)
            sample = _moba_sample_part(page_table, qs, ks, vs, moba_kc, moba_vc, layer=j, grid=grid)
            op, os_ = _fused_attn(page_table, prompt, sample, grid, "moba_attn")
        else:
            lam_init = 0.8 - 0.6 * math.exp(-0.3 * i)
            prompt = _diff_prompt_part(qp, *kv_p[moba], diff_lambda[j], diff_subln[j].reshape(LANES, 1),
                                       batch=batch, layer=j, lam_init=lam_init)
            sample = _diff_sample_part(page_table, qs, ks, vs, diff_lambda[j], diff_subln[j].reshape(1, LANES),
                                       diff_kc, diff_vc, layer=j, lam_init=lam_init, grid=grid)
            op, os_ = _fused_attn(page_table, prompt, sample, grid, "diff_attn")
            op = op.reshape(batch * seq, d)
        g_mlp = norm_mlp[i].reshape(1, d)
        w_up, w_dn = mlp_w_up[i].astype(BF16), mlp_w_down[i].astype(BF16)
        xp = _out_mlp(xp, op, w_o, g_mlp, w_up, w_dn, tm=tm)
        xs = _out_mlp(xs, os_, w_o, g_mlp, w_up, w_dn, tm=n_seq * n_tok)

    def moba_prompt_out(stack):
        return jnp.transpose(stack.reshape(stack.shape[0], batch, a_heads, HEAD_DIM, seq), (0, 1, 4, 2, 3))

    def rows_out(stack, n_b, n_s, heads):
        return stack.reshape(stack.shape[0], n_b, n_s, heads, d // heads)

    return (xp.reshape(batch, seq, d), xs.reshape(n_seq, n_tok, d),
            moba_prompt_out(kv_p[True][0]), moba_prompt_out(kv_p[True][1]),
            rows_out(kv_p[False][0], batch, seq, b_heads), rows_out(kv_p[False][1], batch, seq, b_heads),
            rows_out(kv_s[True][0], n_seq, n_tok, a_heads), rows_out(kv_s[True][1], n_seq, n_tok, a_heads),
            rows_out(kv_s[False][0], n_seq, n_tok, b_heads), rows_out(kv_s[False][1], n_seq, n_tok, b_heads))
```
